```python
import jax, jax.numpy as jnp
from jax import lax
import numpy as np

D_MODEL = 2048
BATCH = 1
SEQ = 16384
DEPTH = 1
DEC_BATCH = 8
DEC_SEQ = 2048
PAST_LEN = 128

N_META = 16
GRID_W = 64
EPS = 1e-6
HEAD_DIM = 64
ATT_WIDTH = D_MODEL // 2
N_Q_HEADS = ATT_WIDTH // HEAD_DIM
N_KV_HEADS = 4
Q_PER_KV = N_Q_HEADS // N_KV_HEADS
Q_BLOCK = 128
ROPE_THETA = 10000.0
SSM_WIDTH = D_MODEL - ATT_WIDTH
SSM_HEAD_DIM = 64
N_SSM_HEADS = SSM_WIDTH // SSM_HEAD_DIM
N_SSM_GROUPS = 2
D_STATE = 128
D_CONV = 5
CHUNK = 128
CONV_DIM = SSM_WIDTH + 2 * N_SSM_GROUPS * D_STATE
MIX_WIDTH = ATT_WIDTH + SSM_WIDTH
D_FF = 5632
KV_WIDTH = N_KV_HEADS * HEAD_DIM
IN_PROJ = ATT_WIDTH + 2 * KV_WIDTH + SSM_WIDTH + CONV_DIM + 2 * N_SSM_HEADS
SPLIT_POINTS = [ATT_WIDTH, ATT_WIDTH + KV_WIDTH, ATT_WIDTH + 2 * KV_WIDTH,
                ATT_WIDTH + 2 * KV_WIDTH + SSM_WIDTH,
                ATT_WIDTH + 2 * KV_WIDTH + SSM_WIDTH + CONV_DIM]

kernel_name = "hymba_bidir_attn_ssd_macaron_encoder"


def rms_norm(x, g):
    xf = x.astype(jnp.float32)
    y = xf * lax.rsqrt(jnp.mean(xf * xf, axis=-1, keepdims=True) + EPS)
    return (y * g.astype(jnp.float32)).astype(x.dtype)


def swiglu(u, w_gate, w_up, w_down):
    return (jax.nn.silu(u @ w_gate) * (u @ w_up)) @ w_down


def axial_rope_tables(n_tok):
    rows = n_tok // GRID_W
    row = jnp.repeat(jnp.arange(rows), GRID_W).astype(jnp.float32)
    col = jnp.tile(jnp.arange(GRID_W), rows).astype(jnp.float32)
    n_freq = HEAD_DIM // 4
    inv_freq = ROPE_THETA ** (-jnp.arange(n_freq, dtype=jnp.float32) / n_freq)
    ang = jnp.stack([row[:, None] * inv_freq, col[:, None] * inv_freq], axis=1)
    ang = jnp.broadcast_to(ang[:, :, None, :], (n_tok, 2, 2, n_freq)).reshape(n_tok, HEAD_DIM)
    ang = jnp.concatenate([jnp.zeros((N_META, HEAD_DIM), jnp.float32), ang], axis=0)
    return jnp.cos(ang), jnp.sin(ang)


def apply_axial_rope(x, cos, sin):
    xs = x.reshape(x.shape[:-1] + (2, 2, HEAD_DIM // 4))
    rot = jnp.stack([-xs[..., 1, :], xs[..., 0, :]], axis=-2).reshape(x.shape)
    return x * cos[None, :, None, :] + rot * sin[None, :, None, :]


def attention_group(q, k, v, cos, sin, q_g, k_g):
    b, L = q.shape[:2]
    q = apply_axial_rope(rms_norm(q, q_g).astype(jnp.float32), cos, sin) * (HEAD_DIM ** -0.5)
    k = apply_axial_rope(rms_norm(k, k_g).astype(jnp.float32), cos, sin).astype(v.dtype)
    q = q.astype(v.dtype).reshape(b, L, N_KV_HEADS, Q_PER_KV, HEAD_DIM)

    def attend(qb):
        s = jnp.einsum('bqkgd,bskd->bkgqs', qb, k, preferred_element_type=jnp.float32)
        p = jax.nn.softmax(s, axis=-1).astype(v.dtype)
        return jnp.einsum('bkgqs,bskd->bqkgd', p, v)

    o_meta = attend(q[:, :N_META])
    n_blk = (L - N_META) // Q_BLOCK
    qb = q[:, N_META:].reshape(b, n_blk, Q_BLOCK, N_KV_HEADS, Q_PER_KV, HEAD_DIM).swapaxes(0, 1)
    o = lax.map(attend, qb).swapaxes(0, 1).reshape(b, L - N_META, N_KV_HEADS, Q_PER_KV, HEAD_DIM)
    return jnp.concatenate([o_meta, o], axis=1).reshape(b, L, ATT_WIDTH)


def centred_dwconv(u, w, bias):
    pad = D_CONV // 2
    L = u.shape[1]
    up = jnp.pad(u, ((0, 0), (pad, pad), (0, 0)))
    out = bias
    for j in range(D_CONV):
        out = out + up[:, j:j + L] * w[j]
    return out


def ssd_scan(x, dt, A, Bm, Cm):
    b, l, h, p = x.shape
    g, n = Bm.shape[2], Bm.shape[3]
    r = h // g
    c = l // CHUNK
    xr = (x * dt[..., None]).reshape(b, c, CHUNK, g, r, p)
    a = (dt * A).reshape(b, c, CHUNK, g, r).transpose(0, 3, 4, 1, 2)
    a_cum = jnp.cumsum(a, axis=-1)
    Bc = Bm.reshape(b, c, CHUNK, g, n)
    Cc = Cm.reshape(b, c, CHUNK, g, n)
    causal = jnp.tril(jnp.ones((CHUNK, CHUNK), dtype=bool))
    seg = a_cum[..., :, None] - a_cum[..., None, :]
    Lmat = jnp.exp(jnp.where(causal, seg, -jnp.inf))
    CB = jnp.einsum('bclgn,bcsgn->bcgls', Cc, Bc)
    y_diag = jnp.einsum('bcgls,bgrcls,bcsgrp->bclgrp', CB, Lmat, xr)
    decay_states = jnp.exp(a_cum[..., -1:] - a_cum)
    states = jnp.einsum('bcsgn,bgrcs,bcsgrp->bcgrpn', Bc, decay_states, xr)
    chunk_decay = jnp.exp(a_cum[..., -1])

    def step(carry, inp):
        s_c, d_c = inp
        return carry * d_c[..., None, None] + s_c, carry

    init = jnp.zeros((b, g, r, p, n), jnp.float32)
    _, prev = lax.scan(step, init, (jnp.moveaxis(states, 1, 0), jnp.moveaxis(chunk_decay, -1, 0)))
    prev = jnp.moveaxis(prev, 0, 1)
    y_off = jnp.einsum('bclgn,bcgrpn,bgrcl->bclgrp', Cc, prev, jnp.exp(a_cum))
    return (y_diag + y_off).reshape(b, l, h, p)


def ssd_group(z, xbc, dt_raw, conv_w, conv_b, a_log, dt_bias, d_skip, norm_g):
    b, L = z.shape[:2]
    xbc = jax.nn.silu(centred_dwconv(xbc, conv_w, conv_b)).astype(jnp.float32)
    xs, Bm, Cm = jnp.split(xbc, [SSM_WIDTH, SSM_WIDTH + N_SSM_GROUPS * D_STATE], axis=-1)
    xs = xs.reshape(b, L, N_SSM_HEADS, SSM_HEAD_DIM)
    Bm = Bm.reshape(b, L, N_SSM_GROUPS, D_STATE)
    Cm = Cm.reshape(b, L, N_SSM_GROUPS, D_STATE)
    dt = jax.nn.softplus(dt_raw.astype(jnp.float32).reshape(b, L, 2, N_SSM_HEADS) + dt_bias.astype(jnp.float32))
    A = -jnp.exp(a_log.astype(jnp.float32))
    lead = CHUNK - N_META
    padf = lambda t: jnp.pad(t, ((0, 0), (lead, 0)) + ((0, 0),) * (t.ndim - 2))
    xp, Bp, Cp, dtp = padf(xs), padf(Bm), padf(Cm), padf(dt)
    flip = lambda t: jnp.flip(t, axis=1)
    y_f = ssd_scan(xp, dtp[:, :, 0], A[0], Bp, Cp)
    y_b = flip(ssd_scan(flip(xp), flip(dtp[:, :, 1]), A[1], flip(Bp), flip(Cp)))
    y = (y_f + y_b)[:, lead:] + d_skip.astype(jnp.float32)[:, None] * xs
    y = y.reshape(b, L, SSM_WIDTH) * jax.nn.silu(z.astype(jnp.float32))
    y = rms_norm(y.reshape(b, L, N_SSM_GROUPS, SSM_WIDTH // N_SSM_GROUPS),
                 norm_g.reshape(N_SSM_GROUPS, SSM_WIDTH // N_SSM_GROUPS))
    return y.reshape(b, L, SSM_WIDTH).astype(z.dtype)


def encoder_layer(h, cos, sin, p):
    b, L, _ = h.shape
    u = rms_norm(h, p['ff1_norm_pre'])
    h = h + 0.5 * rms_norm(swiglu(u, p['ff1_w_gate'], p['ff1_w_up'], p['ff1_w_down']), p['ff1_norm_post'])
    u = rms_norm(h, p['mix_norm_pre'])
    q, k, v, z, xbc, dt_raw = jnp.split(u @ p['w_in'], SPLIT_POINTS, axis=-1)
    q = q.reshape(b, L, N_Q_HEADS, HEAD_DIM)
    k = k.reshape(b, L, N_KV_HEADS, HEAD_DIM)
    v = v.reshape(b, L, N_KV_HEADS, HEAD_DIM)
    o_att = attention_group(q, k, v, cos, sin, p['q_norm'], p['k_norm'])
    o_ssm = ssd_group(z, xbc, dt_raw, p['conv_w'], p['conv_b'], p['a_log'], p['dt_bias'], p['d_skip'], p['ssm_norm'])
    mix = jnp.concatenate([o_att, o_ssm], axis=-1) @ p['w_out']
    h = h + rms_norm(mix, p['mix_norm_post'])
    u = rms_norm(h, p['ff2_norm_pre'])
    h = h + 0.5 * rms_norm(swiglu(u, p['ff2_w_gate'], p['ff2_w_up'], p['ff2_w_down']), p['ff2_norm_post'])
    return h


def run_trunk(x, meta_tokens, layers):
    b, n_tok, _ = x.shape
    cos, sin = axial_rope_tables(n_tok)
    meta = jnp.broadcast_to(meta_tokens.astype(x.dtype)[None], (b, N_META, D_MODEL))
    h = jnp.concatenate([meta, x], axis=1)
    for i in range(DEPTH):
        h = encoder_layer(h, cos, sin, {name: w[i] for name, w in layers.items()})
    return h[:, N_META:]


def setup_inputs(seed: int = 0) -> dict:
    key = jax.random.key(seed)
    ks = jax.random.split(key, 32)
    f32 = jnp.float32
    nrm = lambda k, shape, scale: jax.random.normal(k, shape, f32) * scale
    gain = lambda k, shape: 1.0 + 0.05 * jax.random.normal(k, shape, f32)
    dt0 = jnp.exp(jax.random.uniform(ks[12], (DEPTH, 2, N_SSM_HEADS), f32, np.log(1e-3), np.log(1e-1)))
    return {
        'x_prompt': jax.random.normal(ks[0], (BATCH, SEQ, D_MODEL), f32),
        'x_sample': jax.random.normal(ks[1], (DEC_BATCH, DEC_SEQ, D_MODEL), f32),
        'meta_tokens': nrm(ks[2], (N_META, D_MODEL), 1.0),
        'ff1_norm_pre': gain(ks[3], (DEPTH, D_MODEL)),
        'ff1_w_gate': nrm(ks[4], (DEPTH, D_MODEL, D_FF), D_MODEL ** -0.5),
        'ff1_w_up': nrm(ks[5], (DEPTH, D_MODEL, D_FF), D_MODEL ** -0.5),
        'ff1_w_down': nrm(ks[6], (DEPTH, D_FF, D_MODEL), D_FF ** -0.5),
        'ff1_norm_post': gain(ks[7], (DEPTH, D_MODEL)),
        'mix_norm_pre': gain(ks[8], (DEPTH, D_MODEL)),
        'w_in': nrm(ks[9], (DEPTH, D_MODEL, IN_PROJ), D_MODEL ** -0.5),
        'conv_w': nrm(ks[10], (DEPTH, D_CONV, CONV_DIM), D_CONV ** -0.5),
        'conv_b': nrm(ks[11], (DEPTH, CONV_DIM), 0.02),
        'a_log': jnp.log(jax.random.uniform(ks[13], (DEPTH, 2, N_SSM_HEADS), f32, 1.0, 16.0)),
        'dt_bias': dt0 + jnp.log(-jnp.expm1(-dt0)),
        'd_skip': gain(ks[14], (DEPTH, N_SSM_HEADS)),
        'q_norm': gain(ks[15], (DEPTH, HEAD_DIM)),
        'k_norm': gain(ks[16], (DEPTH, HEAD_DIM)),
        'ssm_norm': gain(ks[17], (DEPTH, SSM_WIDTH)),
        'w_out': nrm(ks[18], (DEPTH, MIX_WIDTH, D_MODEL), MIX_WIDTH ** -0.5),
        'mix_norm_post': gain(ks[19], (DEPTH, D_MODEL)),
        'ff2_norm_pre': gain(ks[20], (DEPTH, D_MODEL)),
        'ff2_w_gate': nrm(ks[21], (DEPTH, D_MODEL, D_FF), D_MODEL ** -0.5),
        'ff2_w_up': nrm(ks[22], (DEPTH, D_MODEL, D_FF), D_MODEL ** -0.5),
        'ff2_w_down': nrm(ks[23], (DEPTH, D_FF, D_MODEL), D_FF ** -0.5),
        'ff2_norm_post': gain(ks[24], (DEPTH, D_MODEL)),
    }


def reference(x_prompt, x_sample, meta_tokens, ff1_norm_pre, ff1_w_gate, ff1_w_up, ff1_w_down, ff1_norm_post,
              mix_norm_pre, w_in, conv_w, conv_b, a_log, dt_bias, d_skip, q_norm, k_norm, ssm_norm, w_out,
              mix_norm_post, ff2_norm_pre, ff2_w_gate, ff2_w_up, ff2_w_down, ff2_norm_post):
    layers = {
        'ff1_norm_pre': ff1_norm_pre, 'ff1_w_gate': ff1_w_gate, 'ff1_w_up': ff1_w_up,
        'ff1_w_down': ff1_w_down, 'ff1_norm_post': ff1_norm_post,
        'mix_norm_pre': mix_norm_pre, 'w_in': w_in, 'conv_w': conv_w, 'conv_b': conv_b,
        'a_log': a_log, 'dt_bias': dt_bias, 'd_skip': d_skip, 'q_norm': q_norm, 'k_norm': k_norm,
        'ssm_norm': ssm_norm, 'w_out': w_out, 'mix_norm_post': mix_norm_post,
        'ff2_norm_pre': ff2_norm_pre, 'ff2_w_gate': ff2_w_gate, 'ff2_w_up': ff2_w_up,
        'ff2_w_down': ff2_w_down, 'ff2_norm_post': ff2_norm_post,
    }
    y_prompt = run_trunk(x_prompt, meta_tokens, layers)
    y_sample = run_trunk(x_sample, meta_tokens, layers)
    return (y_prompt, y_sample)
```

```python
import functools
import math

import jax
import jax.numpy as jnp
from jax import lax
from jax.experimental import pallas as pl
from jax.experimental.pallas import tpu as pltpu

F32 = jnp.float32
BF16 = jnp.bfloat16

D_MODEL = 2048
D_FF = 5632
N_META = 16
GRID_W = 64
EPS = 1e-6
HEAD_DIM = 64
ATT_WIDTH = 1024
N_Q_HEADS = 16
N_KV_HEADS = 4
Q_PER_KV = 4
ROPE_THETA = 10000.0
SSM_WIDTH = 1024
N_SSM_HEADS = 16
N_SSM_GROUPS = 2
HEADS_PER_GROUP = N_SSM_HEADS // N_SSM_GROUPS
D_STATE = 128
D_CONV = 5
CHUNK = 128
CONV_DIM = SSM_WIDTH + 2 * N_SSM_GROUPS * D_STATE
KV_WIDTH = N_KV_HEADS * HEAD_DIM
QKV_WIDTH = ATT_WIDTH + 2 * KV_WIDTH
DT_LANES = 128
SSM_PROJ = SSM_WIDTH + CONV_DIM + DT_LANES

META_ROWS = CHUNK
META_LEAD = CHUNK - N_META
QK_AUG = 128
V_AUG = 80
MASK_NEG = -1e30
CONV_HALO = 8

FF_TILE = 512
ROW_TILE = 512
ATT_TQ = 512
ATT_TK = 512
VMEM_LIMIT = 56 * 1024 * 1024


def _cparams(sem):
    return pltpu.CompilerParams(dimension_semantics=sem, vmem_limit_bytes=VMEM_LIMIT)


def _rms(x, gain):
    ms = jnp.mean(x * x, axis=-1, keepdims=True)
    return x * lax.rsqrt(ms + EPS) * gain


def _ffn_body(x_ref, gpre_ref, wg_ref, wu_ref, wd_ref, gpost_ref, o_ref, u_ref):
    j = pl.program_id(1)

    @pl.when(j == 0)
    def _():
        u_ref[...] = _rms(x_ref[...], gpre_ref[...]).astype(BF16)
        o_ref[...] = jnp.zeros_like(o_ref)

    u = u_ref[...]
    g = jnp.dot(u, wg_ref[...], preferred_element_type=F32)
    up = jnp.dot(u, wu_ref[...], preferred_element_type=F32)
    act = (g * jax.nn.sigmoid(g) * up).astype(BF16)
    o_ref[...] += jnp.dot(act, wd_ref[...], preferred_element_type=F32)

    @pl.when(j == pl.num_programs(1) - 1)
    def _():
        o_ref[...] = x_ref[...] + 0.5 * _rms(o_ref[...], gpost_ref[...])


def _ffn(x, gpre, wg, wu, wd, gpost, *, tm):
    n = x.shape[0]
    return pl.pallas_call(
        _ffn_body,
        grid=(n // tm, D_FF // FF_TILE),
        in_specs=[
            pl.BlockSpec((tm, D_MODEL), lambda i, j: (i, 0)),
            pl.BlockSpec((1, D_MODEL), lambda i, j: (0, 0)),
            pl.BlockSpec((D_MODEL, FF_TILE), lambda i, j: (0, j)),
            pl.BlockSpec((D_MODEL, FF_TILE), lambda i, j: (0, j)),
            pl.BlockSpec((FF_TILE, D_MODEL), lambda i, j: (j, 0)),
            pl.BlockSpec((1, D_MODEL), lambda i, j: (0, 0)),
        ],
        out_specs=pl.BlockSpec((tm, D_MODEL), lambda i, j: (i, 0)),
        out_shape=jax.ShapeDtypeStruct((n, D_MODEL), F32),
        scratch_shapes=[pltpu.VMEM((tm, D_MODEL), BF16)],
        compiler_params=_cparams(("parallel", "arbitrary")),
        name="ffn",
    )(x, gpre, wg, wu, wd, gpost)


def _inproj_body(h_ref, g_ref, wqkv_t_ref, wssm_ref, cos_ref, sin_ref, gq_ref, gk_ref, mask_ref,
                 q_t_ref, k_ref, v_t_ref, z_ref, xbc_ref, dt_ref):
    tm = h_ref.shape[0]
    u = _rms(h_ref[...], g_ref[...]).astype(BF16)
    p_t = lax.dot_general(wqkv_t_ref[...], u, (((1,), (1,)), ((), ())), preferred_element_type=F32)
    ps = jnp.dot(u, wssm_ref[...], preferred_element_type=F32)
    z_ref[...] = ps[:, :SSM_WIDTH]
    xbc_ref[...] = ps[:, SSM_WIDTH:SSM_WIDTH + CONV_DIM]
    dt_ref[...] = ps[:, SSM_WIDTH + CONV_DIM:]

    cos_t = cos_ref[...]
    sin_t = sin_ref[...]

    def norm_rope(x, gain):
        ss = jnp.sum(x * x, axis=0, keepdims=True)
        xg = x * lax.rsqrt(ss * (1.0 / HEAD_DIM) + EPS) * gain
        xs = jnp.concatenate([xg[16:32], xg[0:16], xg[48:64], xg[32:48]], axis=0)
        return xg * cos_t + xs * sin_t

    def first_row(rows, value):
        row = lax.broadcasted_iota(jnp.int32, (rows, tm), 0)
        return jnp.where(row == 0, value, 0.0)

    q_tail = first_row(QK_AUG - HEAD_DIM, 1.0).astype(BF16)
    k_tail = first_row(QK_AUG - HEAD_DIM, jnp.broadcast_to(mask_ref[...], (QK_AUG - HEAD_DIM, tm)))
    v_tail = first_row(V_AUG - HEAD_DIM, 1.0).astype(BF16)
    gq = gq_ref[...]
    gk = gk_ref[...]
    q_scale = HEAD_DIM ** -0.5 * math.log2(math.e)
    for h in range(N_Q_HEADS):
        q = norm_rope(p_t[HEAD_DIM * h:HEAD_DIM * (h + 1)], gq) * q_scale
        q_t_ref[h, 0:HEAD_DIM, :] = q.astype(BF16)
        q_t_ref[h, HEAD_DIM:QK_AUG, :] = q_tail
    for h in range(N_KV_HEADS):
        k = norm_rope(p_t[ATT_WIDTH + HEAD_DIM * h:ATT_WIDTH + HEAD_DIM * (h + 1)], gk)
        k_aug_t = jnp.concatenate([k, k_tail], axis=0)
        k_ref[h] = k_aug_t.T.astype(BF16)
        v0 = ATT_WIDTH + KV_WIDTH + HEAD_DIM * h
        v_t_ref[h, 0:HEAD_DIM, :] = p_t[v0:v0 + HEAD_DIM].astype(BF16)
        v_t_ref[h, HEAD_DIM:V_AUG, :] = v_tail


def _inproj(h, gain, wqkv_t, wssm, cos_t, sin_t, gq, gk, mask, *, tm):
    n = h.shape[0]
    n_tab = cos_t.shape[1] // tm
    const = lambda i: (0, 0)
    return pl.pallas_call(
        _inproj_body,
        grid=(n // tm,),
        in_specs=[
            pl.BlockSpec((tm, D_MODEL), lambda i: (i, 0)),
            pl.BlockSpec((1, D_MODEL), const),
            pl.BlockSpec((QKV_WIDTH, D_MODEL), const, pipeline_mode=pl.Buffered(1)),
            pl.BlockSpec((D_MODEL, SSM_PROJ), const, pipeline_mode=pl.Buffered(1)),
            pl.BlockSpec((HEAD_DIM, tm), lambda i: (0, i % n_tab)),
            pl.BlockSpec((HEAD_DIM, tm), lambda i: (0, i % n_tab)),
            pl.BlockSpec((HEAD_DIM, 1), const),
            pl.BlockSpec((HEAD_DIM, 1), const),
            pl.BlockSpec((1, tm), lambda i: (0, i % n_tab)),
        ],
        out_specs=[
            pl.BlockSpec((N_Q_HEADS, QK_AUG, tm), lambda i: (0, 0, i)),
            pl.BlockSpec((N_KV_HEADS, tm, QK_AUG), lambda i: (0, i, 0)),
            pl.BlockSpec((N_KV_HEADS, V_AUG, tm), lambda i: (0, 0, i)),
            pl.BlockSpec((tm, SSM_WIDTH), lambda i: (i, 0)),
            pl.BlockSpec((tm, CONV_DIM), lambda i: (i, 0)),
            pl.BlockSpec((tm, DT_LANES), lambda i: (i, 0)),
        ],
        out_shape=[
            jax.ShapeDtypeStruct((N_Q_HEADS, QK_AUG, n), BF16),
            jax.ShapeDtypeStruct((N_KV_HEADS, n, QK_AUG), BF16),
            jax.ShapeDtypeStruct((N_KV_HEADS, V_AUG, n), BF16),
            jax.ShapeDtypeStruct((n, SSM_WIDTH), F32),
            jax.ShapeDtypeStruct((n, CONV_DIM), F32),
            jax.ShapeDtypeStruct((n, DT_LANES), F32),
        ],
        compiler_params=_cparams(("parallel",)),
        name="in_proj",
    )(h, gain, wqkv_t, wssm, cos_t, sin_t, gq, gk, mask)


def _attn_body(q_t_ref, k_ref, v_t_ref, km_ref, vm_ref, o_ref, m_ref, acc_ref):
    kj = pl.program_id(3)

    def step(k_blk, v_t_blk, first):
        for h in range(Q_PER_KV):
            s = jnp.dot(k_blk, q_t_ref[h], preferred_element_type=F32)
            m_loc = jnp.max(s, axis=0, keepdims=True)
            if first:
                m_new = m_loc
            else:
                m_old = m_ref[h]
                m_new = jnp.maximum(m_old, m_loc)
            p = jnp.exp2(s - m_new).astype(BF16)
            pv = jnp.dot(v_t_blk, p, preferred_element_type=F32)
            if first:
                acc_ref[h] = pv
            else:
                acc_ref[h] = acc_ref[h] * jnp.exp2(m_old - m_new) + pv
            m_ref[h] = m_new

    @pl.when(kj == 0)
    def _():
        step(km_ref[...], vm_ref[...], True)

    step(k_ref[...], v_t_ref[...], False)

    @pl.when(kj == pl.num_programs(3) - 1)
    def _():
        outs = []
        for h in range(Q_PER_KV):
            acc = acc_ref[h]
            outs.append(acc[0:HEAD_DIM] / acc[HEAD_DIM:HEAD_DIM + 1])
        o_ref[...] = jnp.concatenate(outs, axis=0).T.astype(BF16)


def _attention(q_t, k, v_t, k_meta, v_t_meta, *, batch, seq, tq, tk):
    n = batch * seq
    nq, nk = seq // tq, seq // tk
    return pl.pallas_call(
        _attn_body,
        grid=(batch, N_KV_HEADS, nq, nk),
        in_specs=[
            pl.BlockSpec((Q_PER_KV, QK_AUG, tq), lambda b, g, i, j: (g, 0, b * nq + i)),
            pl.BlockSpec((None, tk, QK_AUG), lambda b, g, i, j: (g, b * nk + j, 0)),
            pl.BlockSpec((None, V_AUG, tk), lambda b, g, i, j: (g, 0, b * nk + j)),
            pl.BlockSpec((None, META_ROWS, QK_AUG), lambda b, g, i, j: (g, 0, 0)),
            pl.BlockSpec((None, V_AUG, META_ROWS), lambda b, g, i, j: (g, 0, 0)),
        ],
        out_specs=pl.BlockSpec((tq, Q_PER_KV * HEAD_DIM), lambda b, g, i, j: (b * nq + i, g)),
        out_shape=jax.ShapeDtypeStruct((n, ATT_WIDTH), BF16),
        scratch_shapes=[
            pltpu.VMEM((Q_PER_KV, 1, tq), F32),
            pltpu.VMEM((Q_PER_KV, V_AUG, tq), F32),
        ],
        compiler_params=_cparams(("parallel", "parallel", "parallel", "arbitrary")),
        name="attention",
    )(q_t, k, v_t, k_meta, v_t_meta)


def _split3(x):
    hi = x.astype(BF16)
    r1 = x - hi.astype(F32)
    mid = r1.astype(BF16)
    lo = (r1 - mid.astype(F32)).astype(BF16)
    return hi, mid, lo


def _tri_dot(tri, x):
    hi, mid, lo = _split3(x)
    return (jnp.dot(tri, hi, preferred_element_type=F32) + jnp.dot(tri, mid, preferred_element_type=F32)
            + jnp.dot(tri, lo, preferred_element_type=F32))


def _tri_masks():
    r = lax.broadcasted_iota(jnp.int32, (CHUNK, CHUNK), 0)
    c = lax.broadcasted_iota(jnp.int32, (CHUNK, CHUNK), 1)
    return c <= r, c >= r


def _dt_comb(dt_raw, dtb_ref, alog_ref, valid_from=None):
    x = dt_raw + dtb_ref[...]
    sp = jnp.maximum(x, 0.0) + jnp.log(1.0 + jnp.exp(-jnp.abs(x)))
    lane = lax.broadcasted_iota(jnp.int32, sp.shape, 1)
    dtv = jnp.where(lane < 2 * N_SSM_HEADS, sp, 0.0)
    if valid_from is not None:
        row = lax.broadcasted_iota(jnp.int32, sp.shape, 0)
        dtv = jnp.where(row >= valid_from, dtv, 0.0)
    lane1 = lax.broadcasted_iota(jnp.int32, (1, DT_LANES), 1)
    a_mult = jnp.where((lane1 >= 2 * N_SSM_HEADS) & (lane1 < 4 * N_SSM_HEADS), -jnp.exp(alog_ref[...]), 0.0)
    return dtv + pltpu.roll(dtv, 2 * N_SSM_HEADS, axis=1) * a_mult


def _conv_silu(ext_ref, cw_ref, cb_ref):
    acc = cb_ref[...] + ext_ref[pl.ds(CONV_HALO - D_CONV // 2, CHUNK), :] * cw_ref[0:1, :]
    for j in range(1, D_CONV):
        acc = acc + ext_ref[pl.ds(CONV_HALO - D_CONV // 2 + j, CHUNK), :] * cw_ref[j:j + 1, :]
    return acc * jax.nn.sigmoid(acc)


DT_F, DT_B = 0, N_SSM_HEADS
A_F, A_B = 2 * N_SSM_HEADS, 3 * N_SSM_HEADS


def _chunk_states(xs_t, comb_t, cum_t, b_mat, dt_row0, a_row0, end_col):
    dec = jnp.exp(cum_t[a_row0:a_row0 + N_SSM_HEADS, end_col:end_col + 1] - cum_t[a_row0:a_row0 + N_SSM_HEADS])
    out = []
    for g in range(N_SSM_GROUPS):
        rows = []
        for hh in range(HEADS_PER_GROUP):
            h = g * HEADS_PER_GROUP + hh
            xr = xs_t[HEAD_DIM * h:HEAD_DIM * (h + 1)] * (comb_t[dt_row0 + h:dt_row0 + h + 1] * dec[h:h + 1])
            rows.append(xr.astype(BF16))
        xrd = jnp.concatenate(rows, axis=0)
        out.append(jnp.dot(xrd, b_mat[:, D_STATE * g:D_STATE * (g + 1)].astype(BF16), preferred_element_type=F32))
    return out


def _update_state(state_ref, contrib, cum_t, a_row0, end_col):
    cd = jnp.exp(cum_t[a_row0:a_row0 + N_SSM_HEADS, end_col:end_col + 1])
    for g in range(N_SSM_GROUPS):
        for hh in range(HEADS_PER_GROUP):
            h = g * HEADS_PER_GROUP + hh
            sl = slice(HEAD_DIM * h, HEAD_DIM * (h + 1))
            state_ref[sl, :] = state_ref[sl, :] * cd[h:h + 1] + contrib[g][HEAD_DIM * hh:HEAD_DIM * (hh + 1)]


def _ssd_fwd_body(xc_ref, xp_ref, xn_ref, dt_ref, xm_ref, dtm_ref, cw_ref, cb_ref, dtb_ref, alog_ref, dskip_ref,
                  ypart_ref, xact_ref, state_ref, ext_ref):
    t = pl.program_id(1)
    ltri, utri = _tri_masks()
    ltri_b = jnp.where(ltri, 1.0, 0.0).astype(BF16)
    utri_b = jnp.where(utri, 1.0, 0.0).astype(BF16)

    @pl.when(t == 0)
    def _():
        ext_ref[0:CONV_HALO, :] = jnp.zeros((CONV_HALO, CONV_DIM), F32)
        ext_ref[CONV_HALO:CONV_HALO + CHUNK, :] = xm_ref[...]
        ext_ref[CONV_HALO + CHUNK:, :] = xc_ref[0:CONV_HALO, :]
        xact = _conv_silu(ext_ref, cw_ref, cb_ref)
        comb = _dt_comb(dtm_ref[...], dtb_ref, alog_ref, valid_from=META_LEAD)
        pc_t = _tri_dot(ltri_b, comb).T
        contrib = _chunk_states(xact[:, :SSM_WIDTH].T, comb.T, pc_t, xact[:, SSM_WIDTH:SSM_WIDTH + N_SSM_GROUPS * D_STATE],
                                DT_F, A_F, CHUNK - 1)
        for g in range(N_SSM_GROUPS):
            state_ref[HEADS_PER_GROUP * HEAD_DIM * g:HEADS_PER_GROUP * HEAD_DIM * (g + 1), :] = contrib[g]

    ext_ref[0:CONV_HALO, :] = jnp.where(t == 0, xm_ref[CHUNK - CONV_HALO:, :], xp_ref[...])
    ext_ref[CONV_HALO:CONV_HALO + CHUNK, :] = xc_ref[...]
    ext_ref[CONV_HALO + CHUNK:, :] = jnp.where(t == pl.num_programs(1) - 1, 0.0, xn_ref[...])
    xact = _conv_silu(ext_ref, cw_ref, cb_ref)
    xact_ref[...] = xact
    xs = xact[:, :SSM_WIDTH]
    b_mat = xact[:, SSM_WIDTH:SSM_WIDTH + N_SSM_GROUPS * D_STATE]
    c_mat = xact[:, SSM_WIDTH + N_SSM_GROUPS * D_STATE:]

    comb = _dt_comb(dt_ref[...], dtb_ref, alog_ref)
    pc = _tri_dot(ltri_b, comb)
    sc = _tri_dot(utri_b, comb)
    pc_t, sc_t, comb_t = pc.T, sc.T, comb.T
    xs_t = xs.T
    exp_p = jnp.exp(pc_t[A_F:A_F + N_SSM_HEADS])

    y_rows = []
    for g in range(N_SSM_GROUPS):
        bg = b_mat[:, D_STATE * g:D_STATE * (g + 1)].astype(BF16)
        cg_t = c_mat[:, D_STATE * g:D_STATE * (g + 1)].T.astype(BF16)
        cb_t = jnp.dot(bg, cg_t, preferred_element_type=F32)
        cb_f = jnp.where(utri, cb_t, 0.0)
        cb_b = jnp.where(ltri, cb_t, 0.0)
        st = state_ref[HEADS_PER_GROUP * HEAD_DIM * g:HEADS_PER_GROUP * HEAD_DIM * (g + 1), :].astype(BF16)
        yo_t = jnp.dot(st, cg_t, preferred_element_type=F32)
        for hh in range(HEADS_PER_GROUP):
            h = g * HEADS_PER_GROUP + hh
            ef = jnp.exp(jnp.minimum(pc_t[A_F + h:A_F + h + 1] - pc[:, A_F + h:A_F + h + 1], 0.0))
            eb = jnp.exp(jnp.minimum(sc_t[A_B + h:A_B + h + 1] - sc[:, A_B + h:A_B + h + 1], 0.0))
            m_t = jnp.concatenate([(cb_f * ef).astype(BF16), (cb_b * eb).astype(BF16)], axis=0)
            x_h = xs_t[HEAD_DIM * h:HEAD_DIM * (h + 1)]
            xr = jnp.concatenate([(x_h * comb_t[DT_F + h:DT_F + h + 1]).astype(BF16),
                                  (x_h * comb_t[DT_B + h:DT_B + h + 1]).astype(BF16)], axis=1)
            y_h = jnp.dot(xr, m_t, preferred_element_type=F32)
            y_rows.append(y_h + yo_t[HEAD_DIM * hh:HEAD_DIM * (hh + 1)] * exp_p[h:h + 1])
    y_t = jnp.concatenate(y_rows, axis=0)
    ypart_ref[...] = y_t.T + xs * dskip_ref[...]

    contrib = _chunk_states(xs_t, comb_t, pc_t, b_mat, DT_F, A_F, CHUNK - 1)
    _update_state(state_ref, contrib, pc_t, A_F, CHUNK - 1)


def _ssd_bwd_body(xact_ref, dt_ref, ypart_ref, z_ref, dtb_ref, alog_ref, gnorm_ref, o_ref, state_ref):
    t = pl.program_id(1)
    _, utri = _tri_masks()
    utri_b = jnp.where(utri, 1.0, 0.0).astype(BF16)

    @pl.when(t == 0)
    def _():
        state_ref[...] = jnp.zeros_like(state_ref)

    xact = xact_ref[...]
    xs = xact[:, :SSM_WIDTH]
    b_mat = xact[:, SSM_WIDTH:SSM_WIDTH + N_SSM_GROUPS * D_STATE]
    c_mat = xact[:, SSM_WIDTH + N_SSM_GROUPS * D_STATE:]
    comb = _dt_comb(dt_ref[...], dtb_ref, alog_ref)
    sc_t = _tri_dot(utri_b, comb).T
    comb_t = comb.T
    exp_s = jnp.exp(sc_t[A_B:A_B + N_SSM_HEADS])

    y_rows = []
    for g in range(N_SSM_GROUPS):
        cg_t = c_mat[:, D_STATE * g:D_STATE * (g + 1)].T.astype(BF16)
        st = state_ref[HEADS_PER_GROUP * HEAD_DIM * g:HEADS_PER_GROUP * HEAD_DIM * (g + 1), :].astype(BF16)
        yo_t = jnp.dot(st, cg_t, preferred_element_type=F32)
        for hh in range(HEADS_PER_GROUP):
            h = g * HEADS_PER_GROUP + hh
            y_rows.append(yo_t[HEAD_DIM * hh:HEAD_DIM * (hh + 1)] * exp_s[h:h + 1])
    y = ypart_ref[...] + jnp.concatenate(y_rows, axis=0).T
    z = z_ref[...]
    y = y * (z * jax.nn.sigmoid(z))
    gw = SSM_WIDTH // N_SSM_GROUPS
    for g in range(N_SSM_GROUPS):
        o_ref[:, gw * g:gw * (g + 1)] = _rms(y[:, gw * g:gw * (g + 1)], gnorm_ref[:, gw * g:gw * (g + 1)]).astype(BF16)

    contrib = _chunk_states(xs.T, comb_t, sc_t, b_mat, DT_B, A_B, 0)
    _update_state(state_ref, contrib, sc_t, A_B, 0)


def _ssd(xbc, dt_raw, z, xbc_meta, dt_meta, conv_w, conv_b, dtb, alog, dskip, gnorm, *, batch, seq):
    n = batch * seq
    nc = seq // CHUNK
    hb = CHUNK // CONV_HALO
    const = lambda b, t: (0, 0)
    ypart, xact = pl.pallas_call(
        _ssd_fwd_body,
        grid=(batch, nc),
        in_specs=[
            pl.BlockSpec((CHUNK, CONV_DIM), lambda b, t: (b * nc + t, 0)),
            pl.BlockSpec((CONV_HALO, CONV_DIM), lambda b, t: (jnp.maximum((b * nc + t) * hb - 1, 0), 0)),
            pl.BlockSpec((CONV_HALO, CONV_DIM), lambda b, t: (jnp.minimum((b * nc + t + 1) * hb, n // CONV_HALO - 1), 0)),
            pl.BlockSpec((CHUNK, DT_LANES), lambda b, t: (b * nc + t, 0)),
            pl.BlockSpec((META_ROWS, CONV_DIM), const),
            pl.BlockSpec((META_ROWS, DT_LANES), const),
            pl.BlockSpec((D_CONV, CONV_DIM), const),
            pl.BlockSpec((1, CONV_DIM), const),
            pl.BlockSpec((1, DT_LANES), const),
            pl.BlockSpec((1, DT_LANES), const),
            pl.BlockSpec((1, SSM_WIDTH), const),
        ],
        out_specs=[
            pl.BlockSpec((CHUNK, SSM_WIDTH), lambda b, t: (b * nc + t, 0)),
            pl.BlockSpec((CHUNK, CONV_DIM), lambda b, t: (b * nc + t, 0)),
        ],
        out_shape=[
            jax.ShapeDtypeStruct((n, SSM_WIDTH), F32),
            jax.ShapeDtypeStruct((n, CONV_DIM), F32),
        ],
        scratch_shapes=[
            pltpu.VMEM((N_SSM_HEADS * HEAD_DIM, D_STATE), F32),
            pltpu.VMEM((CHUNK + 2 * CONV_HALO, CONV_DIM), F32),
        ],
        compiler_params=_cparams(("parallel", "arbitrary")),
        name="ssd_fwd",
    )(xbc, xbc, xbc, dt_raw, xbc_meta, dt_meta, conv_w, conv_b, dtb, alog, dskip)

    rev = lambda b, t: (b * nc + nc - 1 - t, 0)
    return pl.pallas_call(
        _ssd_bwd_body,
        grid=(batch, nc),
        in_specs=[
            pl.BlockSpec((CHUNK, CONV_DIM), rev),
            pl.BlockSpec((CHUNK, DT_LANES), rev),
            pl.BlockSpec((CHUNK, SSM_WIDTH), rev),
            pl.BlockSpec((CHUNK, SSM_WIDTH), rev),
            pl.BlockSpec((1, DT_LANES), const),
            pl.BlockSpec((1, DT_LANES), const),
            pl.BlockSpec((1, SSM_WIDTH), const),
        ],
        out_specs=pl.BlockSpec((CHUNK, SSM_WIDTH), rev),
        out_shape=jax.ShapeDtypeStruct((n, SSM_WIDTH), BF16),
        scratch_shapes=[pltpu.VMEM((N_SSM_HEADS * HEAD_DIM, D_STATE), F32)],
        compiler_params=_cparams(("parallel", "arbitrary")),
        name="ssd_bwd",
    )(xact, dt_raw, ypart, z, dtb, alog, gnorm)


def _outproj_body(oa_ref, os_ref, h_ref, wa_ref, ws_ref, g_ref, o_ref):
    mix = (jnp.dot(oa_ref[...], wa_ref[...], preferred_element_type=F32)
           + jnp.dot(os_ref[...], ws_ref[...], preferred_element_type=F32))
    o_ref[...] = h_ref[...] + _rms(mix, g_ref[...])


def _outproj(o_att, o_ssm, h, w_att, w_ssm, gain, *, tm):
    n = h.shape[0]
    const = lambda i: (0, 0)
    return pl.pallas_call(
        _outproj_body,
        grid=(n // tm,),
        in_specs=[
            pl.BlockSpec((tm, ATT_WIDTH), lambda i: (i, 0)),
            pl.BlockSpec((tm, SSM_WIDTH), lambda i: (i, 0)),
            pl.BlockSpec((tm, D_MODEL), lambda i: (i, 0)),
            pl.BlockSpec((ATT_WIDTH, D_MODEL), const),
            pl.BlockSpec((SSM_WIDTH, D_MODEL), const),
            pl.BlockSpec((1, D_MODEL), const),
        ],
        out_specs=pl.BlockSpec((tm, D_MODEL), lambda i: (i, 0)),
        out_shape=jax.ShapeDtypeStruct((n, D_MODEL), F32),
        compiler_params=_cparams(("parallel",)),
        name="out_proj",
    )(o_att, o_ssm, h, w_att, w_ssm, gain)


def _rope_tables_t(n_tok):
    pos = jnp.arange(n_tok)
    row = (pos // GRID_W).astype(F32)
    col = (pos % GRID_W).astype(F32)
    n_freq = HEAD_DIM // 4
    inv_freq = ROPE_THETA ** (-jnp.arange(n_freq, dtype=F32) / n_freq)
    ang_r = inv_freq[:, None] * row[None, :]
    ang_c = inv_freq[:, None] * col[None, :]
    ang = jnp.concatenate([ang_r, ang_r, ang_c, ang_c], axis=0)
    sign = jnp.concatenate([-jnp.ones((n_freq, 1), F32), jnp.ones((n_freq, 1), F32)] * 2, axis=0)
    return jnp.cos(ang), jnp.sin(ang) * sign


def _tile_of(n, pref):
    return pref if n % pref == 0 else n


def kernel(x_prompt, x_sample, meta_tokens, ff1_norm_pre, ff1_w_gate, ff1_w_up, ff1_w_down, ff1_norm_post,
           mix_norm_pre, w_in, conv_w, conv_b, a_log, dt_bias, d_skip, q_norm, k_norm, ssm_norm, w_out,
           mix_norm_post, ff2_norm_pre, ff2_w_gate, ff2_w_up, ff2_w_down, ff2_norm_post):
    assert ff1_w_gate.shape[0] == 1, "single-layer trunk"
    row2 = lambda v: v.reshape(1, -1).astype(F32)
    ff1 = (row2(ff1_norm_pre[0]), ff1_w_gate[0].astype(BF16), ff1_w_up[0].astype(BF16), ff1_w_down[0].astype(BF16),
           row2(ff1_norm_post[0]))
    ff2 = (row2(ff2_norm_pre[0]), ff2_w_gate[0].astype(BF16), ff2_w_up[0].astype(BF16), ff2_w_down[0].astype(BF16),
           row2(ff2_norm_post[0]))
    w_in0 = w_in[0]
    wqkv_t = w_in0[:, :QKV_WIDTH].T.astype(BF16)
    dt_cols = w_in0[:, QKV_WIDTH + SSM_WIDTH + CONV_DIM:]
    wssm = jnp.concatenate([w_in0[:, QKV_WIDTH:QKV_WIDTH + SSM_WIDTH + CONV_DIM], dt_cols,
                            jnp.zeros((D_MODEL, DT_LANES - dt_cols.shape[1]), F32)], axis=1).astype(BF16)
    g_mix = row2(mix_norm_pre[0])
    gq = q_norm[0].reshape(HEAD_DIM, 1).astype(F32)
    gk = k_norm[0].reshape(HEAD_DIM, 1).astype(F32)
    n_dt = 2 * N_SSM_HEADS
    dtb = jnp.zeros((1, DT_LANES), F32).at[0, :n_dt].set(dt_bias[0].reshape(-1))
    alog = jnp.zeros((1, DT_LANES), F32).at[0, n_dt:2 * n_dt].set(a_log[0].reshape(-1))
    dskip = jnp.repeat(d_skip[0].astype(F32), HEAD_DIM).reshape(1, SSM_WIDTH)
    gnorm = row2(ssm_norm[0])
    cw = conv_w[0].astype(F32)
    cb = row2(conv_b[0])
    w_out_att = w_out[0][:ATT_WIDTH].astype(BF16)
    w_out_ssm = w_out[0][ATT_WIDTH:].astype(BF16)
    g_post = row2(mix_norm_post[0])

    def in_projection(h, cos_t, sin_t, mask, tm):
        return _inproj(h, g_mix, wqkv_t, wssm, cos_t, sin_t, gq, gk, mask, tm=tm)

    x_meta = jnp.concatenate([jnp.zeros((META_LEAD, D_MODEL), F32), meta_tokens.astype(F32)], axis=0)
    h_meta = _ffn(x_meta, *ff1, tm=META_ROWS)
    mask_meta = jnp.where(jnp.arange(META_ROWS) < META_LEAD, MASK_NEG, 0.0).astype(F32).reshape(1, META_ROWS)
    _, k_meta, v_t_meta, _, xbc_meta, dt_meta = in_projection(
        h_meta, jnp.ones((HEAD_DIM, META_ROWS), F32), jnp.zeros((HEAD_DIM, META_ROWS), F32), mask_meta, META_ROWS)

    def trunk(x):
        batch, seq, _ = x.shape
        n = batch * seq
        tm = _tile_of(n, ROW_TILE)
        h1 = _ffn(x.reshape(n, D_MODEL), *ff1, tm=tm)
        cos_t, sin_t = _rope_tables_t(seq)
        tm_in = _tile_of(seq, ROW_TILE)
        q_t, k, v_t, z, xbc, dt_raw = in_projection(h1, cos_t, sin_t, jnp.zeros((1, seq), F32), tm_in)
        o_att = _attention(q_t, k, v_t, k_meta, v_t_meta, batch=batch, seq=seq,
                           tq=_tile_of(seq, ATT_TQ), tk=_tile_of(seq, ATT_TK))
        o_ssm = _ssd(xbc, dt_raw, z, xbc_meta, dt_meta, cw, cb, dtb, alog, dskip, gnorm, batch=batch, seq=seq)
        h2 = _outproj(o_att, o_ssm, h1, w_out_att, w_out_ssm, g_post, tm=tm)
        return _ffn(h2, *ff2, tm=tm).reshape(batch, seq, D_MODEL)

    return trunk(x_prompt), trunk(x_sample)
```

```python
import functools
import math

import jax
import jax.numpy as jnp
from jax import lax
from jax.experimental import pallas as pl
from jax.experimental.pallas import tpu as pltpu

F32 = jnp.float32
BF16 = jnp.bfloat16

D_MODEL = 2048
D_FF = 5632
N_META = 16
GRID_W = 64
EPS = 1e-6
HEAD_DIM = 64
ATT_WIDTH = 1024
N_Q_HEADS = 16
N_KV_HEADS = 4
Q_PER_KV = 4
ROPE_THETA = 10000.0
SSM_WIDTH = 1024
N_SSM_HEADS = 16
N_SSM_GROUPS = 2
HEADS_PER_GROUP = N_SSM_HEADS // N_SSM_GROUPS
D_STATE = 128
D_CONV = 5
CHUNK = 128
CONV_DIM = SSM_WIDTH + 2 * N_SSM_GROUPS * D_STATE
KV_WIDTH = N_KV_HEADS * HEAD_DIM
QKV_WIDTH = ATT_WIDTH + 2 * KV_WIDTH
DT_LANES = 128
SSM_PROJ = SSM_WIDTH + CONV_DIM + DT_LANES

META_ROWS = CHUNK
META_LEAD = CHUNK - N_META
QK_AUG = 128
V_AUG = 80
MASK_NEG = -1e30
CONV_HALO = 8

FF_TILE = 512
ROW_TILE = 512
ATT_TQ = 512
ATT_TK = 512
ATT_RING = 4
VMEM_LIMIT = 56 * 1024 * 1024


def _cparams(sem):
    return pltpu.CompilerParams(dimension_semantics=sem, vmem_limit_bytes=VMEM_LIMIT)


def _rms(x, gain):
    ms = jnp.mean(x * x, axis=-1, keepdims=True)
    return x * lax.rsqrt(ms + EPS) * gain


def _ffn_body(x_ref, gpre_ref, wg_ref, wu_ref, wd_ref, gpost_ref, o_ref, u_ref):
    j = pl.program_id(1)

    @pl.when(j == 0)
    def _():
        u_ref[...] = _rms(x_ref[...], gpre_ref[...]).astype(BF16)
        o_ref[...] = jnp.zeros_like(o_ref)

    u = u_ref[...]
    g = jnp.dot(u, wg_ref[...], preferred_element_type=F32)
    up = jnp.dot(u, wu_ref[...], preferred_element_type=F32)
    act = (g * jax.nn.sigmoid(g) * up).astype(BF16)
    o_ref[...] += jnp.dot(act, wd_ref[...], preferred_element_type=F32)

    @pl.when(j == pl.num_programs(1) - 1)
    def _():
        o_ref[...] = x_ref[...] + 0.5 * _rms(o_ref[...], gpost_ref[...])


def _ffn(x, gpre, wg, wu, wd, gpost, *, tm):
    n = x.shape[0]
    return pl.pallas_call(
        _ffn_body,
        grid=(n // tm, D_FF // FF_TILE),
        in_specs=[
            pl.BlockSpec((tm, D_MODEL), lambda i, j: (i, 0)),
            pl.BlockSpec((1, D_MODEL), lambda i, j: (0, 0)),
            pl.BlockSpec((D_MODEL, FF_TILE), lambda i, j: (0, j)),
            pl.BlockSpec((D_MODEL, FF_TILE), lambda i, j: (0, j)),
            pl.BlockSpec((FF_TILE, D_MODEL), lambda i, j: (j, 0)),
            pl.BlockSpec((1, D_MODEL), lambda i, j: (0, 0)),
        ],
        out_specs=pl.BlockSpec((tm, D_MODEL), lambda i, j: (i, 0)),
        out_shape=jax.ShapeDtypeStruct((n, D_MODEL), F32),
        scratch_shapes=[pltpu.VMEM((tm, D_MODEL), BF16)],
        compiler_params=_cparams(("parallel", "arbitrary")),
        name="ffn",
    )(x, gpre, wg, wu, wd, gpost)


def _inproj_body(h_ref, g_ref, wqkv_t_ref, wssm_ref, cos_ref, sin_ref, gq_ref, gk_ref, mask_ref,
                 q_t_ref, k_ref, v_t_ref, z_ref, xbc_ref, dt_ref):
    tm = h_ref.shape[0]
    u = _rms(h_ref[...], g_ref[...]).astype(BF16)
    p_t = lax.dot_general(wqkv_t_ref[...], u, (((1,), (1,)), ((), ())), preferred_element_type=F32)
    ps = jnp.dot(u, wssm_ref[...], preferred_element_type=F32)
    z_ref[...] = ps[:, :SSM_WIDTH]
    xbc_ref[...] = ps[:, SSM_WIDTH:SSM_WIDTH + CONV_DIM]
    dt_ref[...] = ps[:, SSM_WIDTH + CONV_DIM:]

    cos_t = cos_ref[...]
    sin_t = sin_ref[...]

    def norm_rope(x, gain):
        ss = jnp.sum(x * x, axis=0, keepdims=True)
        xg = x * lax.rsqrt(ss * (1.0 / HEAD_DIM) + EPS) * gain
        xs = jnp.concatenate([xg[16:32], xg[0:16], xg[48:64], xg[32:48]], axis=0)
        return xg * cos_t + xs * sin_t

    def first_row(rows, value):
        row = lax.broadcasted_iota(jnp.int32, (rows, tm), 0)
        return jnp.where(row == 0, value, 0.0)

    q_tail = first_row(QK_AUG - HEAD_DIM, 1.0).astype(BF16)
    k_tail = first_row(QK_AUG - HEAD_DIM, jnp.broadcast_to(mask_ref[...], (QK_AUG - HEAD_DIM, tm)))
    v_tail = first_row(V_AUG - HEAD_DIM, 1.0).astype(BF16)
    gq = gq_ref[...]
    gk = gk_ref[...]
    q_scale = HEAD_DIM ** -0.5 * math.log2(math.e)
    for h in range(N_Q_HEADS):
        q = norm_rope(p_t[HEAD_DIM * h:HEAD_DIM * (h + 1)], gq) * q_scale
        q_t_ref[h, 0:HEAD_DIM, :] = q.astype(BF16)
        q_t_ref[h, HEAD_DIM:QK_AUG, :] = q_tail
    for h in range(N_KV_HEADS):
        k = norm_rope(p_t[ATT_WIDTH + HEAD_DIM * h:ATT_WIDTH + HEAD_DIM * (h + 1)], gk)
        k_aug_t = jnp.concatenate([k, k_tail], axis=0)
        k_ref[h] = k_aug_t.T.astype(BF16)
        v0 = ATT_WIDTH + KV_WIDTH + HEAD_DIM * h
        v_t_ref[h, 0:HEAD_DIM, :] = p_t[v0:v0 + HEAD_DIM].astype(BF16)
        v_t_ref[h, HEAD_DIM:V_AUG, :] = v_tail


def _inproj(h, gain, wqkv_t, wssm, cos_t, sin_t, gq, gk, mask, *, tm):
    n = h.shape[0]
    n_tab = cos_t.shape[1] // tm
    const = lambda i: (0, 0)
    return pl.pallas_call(
        _inproj_body,
        grid=(n // tm,),
        in_specs=[
            pl.BlockSpec((tm, D_MODEL), lambda i: (i, 0)),
            pl.BlockSpec((1, D_MODEL), const),
            pl.BlockSpec((QKV_WIDTH, D_MODEL), const, pipeline_mode=pl.Buffered(1)),
            pl.BlockSpec((D_MODEL, SSM_PROJ), const, pipeline_mode=pl.Buffered(1)),
            pl.BlockSpec((HEAD_DIM, tm), lambda i: (0, i % n_tab)),
            pl.BlockSpec((HEAD_DIM, tm), lambda i: (0, i % n_tab)),
            pl.BlockSpec((HEAD_DIM, 1), const),
            pl.BlockSpec((HEAD_DIM, 1), const),
            pl.BlockSpec((1, tm), lambda i: (0, i % n_tab)),
        ],
        out_specs=[
            pl.BlockSpec((N_Q_HEADS, QK_AUG, tm), lambda i: (0, 0, i)),
            pl.BlockSpec((N_KV_HEADS, tm, QK_AUG), lambda i: (0, i, 0)),
            pl.BlockSpec((N_KV_HEADS, V_AUG, tm), lambda i: (0, 0, i)),
            pl.BlockSpec((tm, SSM_WIDTH), lambda i: (i, 0)),
            pl.BlockSpec((tm, CONV_DIM), lambda i: (i, 0)),
            pl.BlockSpec((tm, DT_LANES), lambda i: (i, 0)),
        ],
        out_shape=[
            jax.ShapeDtypeStruct((N_Q_HEADS, QK_AUG, n), BF16),
            jax.ShapeDtypeStruct((N_KV_HEADS, n, QK_AUG), BF16),
            jax.ShapeDtypeStruct((N_KV_HEADS, V_AUG, n), BF16),
            jax.ShapeDtypeStruct((n, SSM_WIDTH), F32),
            jax.ShapeDtypeStruct((n, CONV_DIM), F32),
            jax.ShapeDtypeStruct((n, DT_LANES), F32),
        ],
        compiler_params=_cparams(("parallel",)),
        name="in_proj",
    )(h, gain, wqkv_t, wssm, cos_t, sin_t, gq, gk, mask)


def _attn_body(q_t_ref, k_ref, v_t_ref, km_ref, vm_ref, o_ref, m_ref, acc_ref, *ring):
    kj = pl.program_id(2)
    s_bufs, ml_bufs = ring[:ATT_RING], ring[ATT_RING:]

    def scores(h, s_ref, ml_ref):
        s = jnp.dot(k_ref[h // Q_PER_KV], q_t_ref[h], preferred_element_type=F32)
        s_ref[...] = s
        ml_ref[...] = jnp.max(s, axis=0, keepdims=True)

    def softmax_pv(h, s_ref, ml_ref):
        m_old = m_ref[h]
        m_new = jnp.maximum(m_old, ml_ref[...])
        p = jnp.exp2(s_ref[...] - m_new).astype(BF16)
        pv = jnp.dot(v_t_ref[h // Q_PER_KV], p, preferred_element_type=F32)
        acc_ref[h] = acc_ref[h] * jnp.exp2(m_old - m_new) + pv
        m_ref[h] = m_new

    @pl.when(kj == 0)
    def _():
        def meta_head(h, carry):
            s = jnp.dot(km_ref[h // Q_PER_KV], q_t_ref[h], preferred_element_type=F32)
            m = jnp.max(s, axis=0, keepdims=True)
            p = jnp.exp2(s - m).astype(BF16)
            acc_ref[h] = jnp.dot(vm_ref[h // Q_PER_KV], p, preferred_element_type=F32)
            m_ref[h] = m
            return carry

        lax.fori_loop(0, N_Q_HEADS, meta_head, 0)

    for h in range(N_Q_HEADS + ATT_RING - 1):
        if h < N_Q_HEADS:
            scores(h, s_bufs[h % ATT_RING], ml_bufs[h % ATT_RING])
        d = h - (ATT_RING - 1)
        if d >= 0:
            softmax_pv(d, s_bufs[d % ATT_RING], ml_bufs[d % ATT_RING])

    @pl.when(kj == pl.num_programs(2) - 1)
    def _():
        for g in range(N_KV_HEADS):
            outs = []
            for hh in range(Q_PER_KV):
                acc = acc_ref[g * Q_PER_KV + hh]
                outs.append(acc[0:HEAD_DIM] / acc[HEAD_DIM:HEAD_DIM + 1])
            w = Q_PER_KV * HEAD_DIM
            o_ref[:, g * w:(g + 1) * w] = jnp.concatenate(outs, axis=0).T.astype(BF16)


def _attention(q_t, k, v_t, k_meta, v_t_meta, *, batch, seq, tq, tk):
    n = batch * seq
    nq, nk = seq // tq, seq // tk
    return pl.pallas_call(
        _attn_body,
        grid=(batch, nq, nk),
        in_specs=[
            pl.BlockSpec((N_Q_HEADS, QK_AUG, tq), lambda b, i, j: (0, 0, b * nq + i)),
            pl.BlockSpec((N_KV_HEADS, tk, QK_AUG), lambda b, i, j: (0, b * nk + j, 0)),
            pl.BlockSpec((N_KV_HEADS, V_AUG, tk), lambda b, i, j: (0, 0, b * nk + j)),
            pl.BlockSpec((N_KV_HEADS, META_ROWS, QK_AUG), lambda b, i, j: (0, 0, 0)),
            pl.BlockSpec((N_KV_HEADS, V_AUG, META_ROWS), lambda b, i, j: (0, 0, 0)),
        ],
        out_specs=pl.BlockSpec((tq, ATT_WIDTH), lambda b, i, j: (b * nq + i, 0)),
        out_shape=jax.ShapeDtypeStruct((n, ATT_WIDTH), BF16),
        scratch_shapes=[
            pltpu.VMEM((N_Q_HEADS, 1, tq), F32),
            pltpu.VMEM((N_Q_HEADS, V_AUG, tq), F32),
        ] + [pltpu.VMEM((tk, tq), F32)] * ATT_RING + [pltpu.VMEM((1, tq), F32)] * ATT_RING,
        compiler_params=_cparams(("parallel", "parallel", "arbitrary")),
        name="attention",
    )(q_t, k, v_t, k_meta, v_t_meta)


def _split3(x):
    hi = x.astype(BF16)
    r1 = x - hi.astype(F32)
    mid = r1.astype(BF16)
    lo = (r1 - mid.astype(F32)).astype(BF16)
    return hi, mid, lo


def _tri_dot(tri, x):
    hi, mid, lo = _split3(x)
    return (jnp.dot(tri, hi, preferred_element_type=F32) + jnp.dot(tri, mid, preferred_element_type=F32)
            + jnp.dot(tri, lo, preferred_element_type=F32))


def _tri_masks():
    r = lax.broadcasted_iota(jnp.int32, (CHUNK, CHUNK), 0)
    c = lax.broadcasted_iota(jnp.int32, (CHUNK, CHUNK), 1)
    return c <= r, c >= r


def _dt_comb(dt_raw, dtb_ref, alog_ref, valid_from=None):
    x = dt_raw + dtb_ref[...]
    sp = jnp.maximum(x, 0.0) + jnp.log(1.0 + jnp.exp(-jnp.abs(x)))
    lane = lax.broadcasted_iota(jnp.int32, sp.shape, 1)
    dtv = jnp.where(lane < 2 * N_SSM_HEADS, sp, 0.0)
    if valid_from is not None:
        row = lax.broadcasted_iota(jnp.int32, sp.shape, 0)
        dtv = jnp.where(row >= valid_from, dtv, 0.0)
    lane1 = lax.broadcasted_iota(jnp.int32, (1, DT_LANES), 1)
    a_mult = jnp.where((lane1 >= 2 * N_SSM_HEADS) & (lane1 < 4 * N_SSM_HEADS), -jnp.exp(alog_ref[...]), 0.0)
    return dtv + pltpu.roll(dtv, 2 * N_SSM_HEADS, axis=1) * a_mult


def _conv_silu(ext_ref, cw_ref, cb_ref):
    acc = cb_ref[...] + ext_ref[pl.ds(CONV_HALO - D_CONV // 2, CHUNK), :] * cw_ref[0:1, :]
    for j in range(1, D_CONV):
        acc = acc + ext_ref[pl.ds(CONV_HALO - D_CONV // 2 + j, CHUNK), :] * cw_ref[j:j + 1, :]
    return acc * jax.nn.sigmoid(acc)


DT_F, DT_B = 0, N_SSM_HEADS
A_F, A_B = 2 * N_SSM_HEADS, 3 * N_SSM_HEADS


def _chunk_states(xs_t, comb_t, cum_t, b_mat, dt_row0, a_row0, end_col):
    dec = jnp.exp(cum_t[a_row0:a_row0 + N_SSM_HEADS, end_col:end_col + 1] - cum_t[a_row0:a_row0 + N_SSM_HEADS])
    out = []
    for g in range(N_SSM_GROUPS):
        rows = []
        for hh in range(HEADS_PER_GROUP):
            h = g * HEADS_PER_GROUP + hh
            xr = xs_t[HEAD_DIM * h:HEAD_DIM * (h + 1)] * (comb_t[dt_row0 + h:dt_row0 + h + 1] * dec[h:h + 1])
            rows.append(xr.astype(BF16))
        xrd = jnp.concatenate(rows, axis=0)
        out.append(jnp.dot(xrd, b_mat[:, D_STATE * g:D_STATE * (g + 1)].astype(BF16), preferred_element_type=F32))
    return out


def _update_state(state_ref, contrib, cum_t, a_row0, end_col):
    cd = jnp.exp(cum_t[a_row0:a_row0 + N_SSM_HEADS, end_col:end_col + 1])
    for g in range(N_SSM_GROUPS):
        for hh in range(HEADS_PER_GROUP):
            h = g * HEADS_PER_GROUP + hh
            sl = slice(HEAD_DIM * h, HEAD_DIM * (h + 1))
            state_ref[sl, :] = state_ref[sl, :] * cd[h:h + 1] + contrib[g][HEAD_DIM * hh:HEAD_DIM * (hh + 1)]


def _ssd_fwd_body(xc_ref, xp_ref, xn_ref, dt_ref, xm_ref, dtm_ref, cw_ref, cb_ref, dtb_ref, alog_ref, dskip_ref,
                  ypart_ref, xact_ref, state_ref, ext_ref):
    t = pl.program_id(1)
    ltri, utri = _tri_masks()
    ltri_b = jnp.where(ltri, 1.0, 0.0).astype(BF16)
    utri_b = jnp.where(utri, 1.0, 0.0).astype(BF16)

    @pl.when(t == 0)
    def _():
        ext_ref[0:CONV_HALO, :] = jnp.zeros((CONV_HALO, CONV_DIM), F32)
        ext_ref[CONV_HALO:CONV_HALO + CHUNK, :] = xm_ref[...]
        ext_ref[CONV_HALO + CHUNK:, :] = xc_ref[0:CONV_HALO, :]
        xact = _conv_silu(ext_ref, cw_ref, cb_ref)
        comb = _dt_comb(dtm_ref[...], dtb_ref, alog_ref, valid_from=META_LEAD)
        pc_t = _tri_dot(ltri_b, comb).T
        contrib = _chunk_states(xact[:, :SSM_WIDTH].T, comb.T, pc_t, xact[:, SSM_WIDTH:SSM_WIDTH + N_SSM_GROUPS * D_STATE],
                                DT_F, A_F, CHUNK - 1)
        for g in range(N_SSM_GROUPS):
            state_ref[HEADS_PER_GROUP * HEAD_DIM * g:HEADS_PER_GROUP * HEAD_DIM * (g + 1), :] = contrib[g]

    ext_ref[0:CONV_HALO, :] = jnp.where(t == 0, xm_ref[CHUNK - CONV_HALO:, :], xp_ref[...])
    ext_ref[CONV_HALO:CONV_HALO + CHUNK, :] = xc_ref[...]
    ext_ref[CONV_HALO + CHUNK:, :] = jnp.where(t == pl.num_programs(1) - 1, 0.0, xn_ref[...])
    xact = _conv_silu(ext_ref, cw_ref, cb_ref)
    xact_ref[...] = xact
    xs = xact[:, :SSM_WIDTH]
    b_mat = xact[:, SSM_WIDTH:SSM_WIDTH + N_SSM_GROUPS * D_STATE]
    c_mat = xact[:, SSM_WIDTH + N_SSM_GROUPS * D_STATE:]

    comb = _dt_comb(dt_ref[...], dtb_ref, alog_ref)
    pc = _tri_dot(ltri_b, comb)
    sc = _tri_dot(utri_b, comb)
    pc_t, sc_t, comb_t = pc.T, sc.T, comb.T
    xs_t = xs.T
    exp_p = jnp.exp(pc_t[A_F:A_F + N_SSM_HEADS])

    y_rows = []
    for g in range(N_SSM_GROUPS):
        bg = b_mat[:, D_STATE * g:D_STATE * (g + 1)].astype(BF16)
        cg_t = c_mat[:, D_STATE * g:D_STATE * (g + 1)].T.astype(BF16)
        cb_t = jnp.dot(bg, cg_t, preferred_element_type=F32)
        cb_f = jnp.where(utri, cb_t, 0.0)
        cb_b = jnp.where(ltri, cb_t, 0.0)
        st = state_ref[HEADS_PER_GROUP * HEAD_DIM * g:HEADS_PER_GROUP * HEAD_DIM * (g + 1), :].astype(BF16)
        yo_t = jnp.dot(st, cg_t, preferred_element_type=F32)
        for hh in range(HEADS_PER_GROUP):
            h = g * HEADS_PER_GROUP + hh
            ef = jnp.exp(jnp.minimum(pc_t[A_F + h:A_F + h + 1] - pc[:, A_F + h:A_F + h + 1], 0.0))
            eb = jnp.exp(jnp.minimum(sc_t[A_B + h:A_B + h + 1] - sc[:, A_B + h:A_B + h + 1], 0.0))
            m_t = jnp.concatenate([(cb_f * ef).astype(BF16), (cb_b * eb).astype(BF16)], axis=0)
            x_h = xs_t[HEAD_DIM * h:HEAD_DIM * (h + 1)]
            xr = jnp.concatenate([(x_h * comb_t[DT_F + h:DT_F + h + 1]).astype(BF16),
                                  (x_h * comb_t[DT_B + h:DT_B + h + 1]).astype(BF16)], axis=1)
            y_h = jnp.dot(xr, m_t, preferred_element_type=F32)
            y_rows.append(y_h + yo_t[HEAD_DIM * hh:HEAD_DIM * (hh + 1)] * exp_p[h:h + 1])
    y_t = jnp.concatenate(y_rows, axis=0)
    ypart_ref[...] = y_t.T + xs * dskip_ref[...]

    contrib = _chunk_states(xs_t, comb_t, pc_t, b_mat, DT_F, A_F, CHUNK - 1)
    _update_state(state_ref, contrib, pc_t, A_F, CHUNK - 1)


def _ssd_bwd_body(xact_ref, dt_ref, ypart_ref, z_ref, dtb_ref, alog_ref, gnorm_ref, o_ref, state_ref):
    t = pl.program_id(1)
    _, utri = _tri_masks()
    utri_b = jnp.where(utri, 1.0, 0.0).astype(BF16)

    @pl.when(t == 0)
    def _():
        state_ref[...] = jnp.zeros_like(state_ref)

    xact = xact_ref[...]
    xs = xact[:, :SSM_WIDTH]
    b_mat = xact[:, SSM_WIDTH:SSM_WIDTH + N_SSM_GROUPS * D_STATE]
    c_mat = xact[:, SSM_WIDTH + N_SSM_GROUPS * D_STATE:]
    comb = _dt_comb(dt_ref[...], dtb_ref, alog_ref)
    sc_t = _tri_dot(utri_b, comb).T
    comb_t = comb.T
    exp_s = jnp.exp(sc_t[A_B:A_B + N_SSM_HEADS])

    y_rows = []
    for g in range(N_SSM_GROUPS):
        cg_t = c_mat[:, D_STATE * g:D_STATE * (g + 1)].T.astype(BF16)
        st = state_ref[HEADS_PER_GROUP * HEAD_DIM * g:HEADS_PER_GROUP * HEAD_DIM * (g + 1), :].astype(BF16)
        yo_t = jnp.dot(st, cg_t, preferred_element_type=F32)
        for hh in range(HEADS_PER_GROUP):
            h = g * HEADS_PER_GROUP + hh
            y_rows.append(yo_t[HEAD_DIM * hh:HEAD_DIM * (hh + 1)] * exp_s[h:h + 1])
    y = ypart_ref[...] + jnp.concatenate(y_rows, axis=0).T
    z = z_ref[...]
    y = y * (z * jax.nn.sigmoid(z))
    gw = SSM_WIDTH // N_SSM_GROUPS
    for g in range(N_SSM_GROUPS):
        o_ref[:, gw * g:gw * (g + 1)] = _rms(y[:, gw * g:gw * (g + 1)], gnorm_ref[:, gw * g:gw * (g + 1)]).astype(BF16)

    contrib = _chunk_states(xs.T, comb_t, sc_t, b_mat, DT_B, A_B, 0)
    _update_state(state_ref, contrib, sc_t, A_B, 0)


def _ssd(xbc, dt_raw, z, xbc_meta, dt_meta, conv_w, conv_b, dtb, alog, dskip, gnorm, *, batch, seq):
    n = batch * seq
    nc = seq // CHUNK
    hb = CHUNK // CONV_HALO
    const = lambda b, t: (0, 0)
    ypart, xact = pl.pallas_call(
        _ssd_fwd_body,
        grid=(batch, nc),
        in_specs=[
            pl.BlockSpec((CHUNK, CONV_DIM), lambda b, t: (b * nc + t, 0)),
            pl.BlockSpec((CONV_HALO, CONV_DIM), lambda b, t: (jnp.maximum((b * nc + t) * hb - 1, 0), 0)),
            pl.BlockSpec((CONV_HALO, CONV_DIM), lambda b, t: (jnp.minimum((b * nc + t + 1) * hb, n // CONV_HALO - 1), 0)),
            pl.BlockSpec((CHUNK, DT_LANES), lambda b, t: (b * nc + t, 0)),
            pl.BlockSpec((META_ROWS, CONV_DIM), const),
            pl.BlockSpec((META_ROWS, DT_LANES), const),
            pl.BlockSpec((D_CONV, CONV_DIM), const),
            pl.BlockSpec((1, CONV_DIM), const),
            pl.BlockSpec((1, DT_LANES), const),
            pl.BlockSpec((1, DT_LANES), const),
            pl.BlockSpec((1, SSM_WIDTH), const),
        ],
        out_specs=[
            pl.BlockSpec((CHUNK, SSM_WIDTH), lambda b, t: (b * nc + t, 0)),
            pl.BlockSpec((CHUNK, CONV_DIM), lambda b, t: (b * nc + t, 0)),
        ],
        out_shape=[
            jax.ShapeDtypeStruct((n, SSM_WIDTH), F32),
            jax.ShapeDtypeStruct((n, CONV_DIM), F32),
        ],
        scratch_shapes=[
            pltpu.VMEM((N_SSM_HEADS * HEAD_DIM, D_STATE), F32),
            pltpu.VMEM((CHUNK + 2 * CONV_HALO, CONV_DIM), F32),
        ],
        compiler_params=_cparams(("parallel", "arbitrary")),
        name="ssd_fwd",
    )(xbc, xbc, xbc, dt_raw, xbc_meta, dt_meta, conv_w, conv_b, dtb, alog, dskip)

    rev = lambda b, t: (b * nc + nc - 1 - t, 0)
    return pl.pallas_call(
        _ssd_bwd_body,
        grid=(batch, nc),
        in_specs=[
            pl.BlockSpec((CHUNK, CONV_DIM), rev),
            pl.BlockSpec((CHUNK, DT_LANES), rev),
            pl.BlockSpec((CHUNK, SSM_WIDTH), rev),
            pl.BlockSpec((CHUNK, SSM_WIDTH), rev),
            pl.BlockSpec((1, DT_LANES), const),
            pl.BlockSpec((1, DT_LANES), const),
            pl.BlockSpec((1, SSM_WIDTH), const),
        ],
        out_specs=pl.BlockSpec((CHUNK, SSM_WIDTH), rev),
        out_shape=jax.ShapeDtypeStruct((n, SSM_WIDTH), BF16),
        scratch_shapes=[pltpu.VMEM((N_SSM_HEADS * HEAD_DIM, D_STATE), F32)],
        compiler_params=_cparams(("parallel", "arbitrary")),
        name="ssd_bwd",
    )(xact, dt_raw, ypart, z, dtb, alog, gnorm)


def _outproj_body(oa_ref, os_ref, h_ref, wa_ref, ws_ref, g_ref, o_ref):
    mix = (jnp.dot(oa_ref[...], wa_ref[...], preferred_element_type=F32)
           + jnp.dot(os_ref[...], ws_ref[...], preferred_element_type=F32))
    o_ref[...] = h_ref[...] + _rms(mix, g_ref[...])


def _outproj(o_att, o_ssm, h, w_att, w_ssm, gain, *, tm):
    n = h.shape[0]
    const = lambda i: (0, 0)
    return pl.pallas_call(
        _outproj_body,
        grid=(n // tm,),
        in_specs=[
            pl.BlockSpec((tm, ATT_WIDTH), lambda i: (i, 0)),
            pl.BlockSpec((tm, SSM_WIDTH), lambda i: (i, 0)),
            pl.BlockSpec((tm, D_MODEL), lambda i: (i, 0)),
            pl.BlockSpec((ATT_WIDTH, D_MODEL), const),
            pl.BlockSpec((SSM_WIDTH, D_MODEL), const),
            pl.BlockSpec((1, D_MODEL), const),
        ],
        out_specs=pl.BlockSpec((tm, D_MODEL), lambda i: (i, 0)),
        out_shape=jax.ShapeDtypeStruct((n, D_MODEL), F32),
        compiler_params=_cparams(("parallel",)),
        name="out_proj",
    )(o_att, o_ssm, h, w_att, w_ssm, gain)


def _rope_tables_t(n_tok):
    pos = jnp.arange(n_tok)
    row = (pos // GRID_W).astype(F32)
    col = (pos % GRID_W).astype(F32)
    n_freq = HEAD_DIM // 4
    inv_freq = ROPE_THETA ** (-jnp.arange(n_freq, dtype=F32) / n_freq)
    ang_r = inv_freq[:, None] * row[None, :]
    ang_c = inv_freq[:, None] * col[None, :]
    ang = jnp.concatenate([ang_r, ang_r, ang_c, ang_c], axis=0)
    sign = jnp.concatenate([-jnp.ones((n_freq, 1), F32), jnp.ones((n_freq, 1), F32)] * 2, axis=0)
    return jnp.cos(ang), jnp.sin(ang) * sign


def _tile_of(n, pref):
    return pref if n % pref == 0 else n


def kernel(x_prompt, x_sample, meta_tokens, ff1_norm_pre, ff1_w_gate, ff1_w_up, ff1_w_down, ff1_norm_post,
           mix_norm_pre, w_in, conv_w, conv_b, a_log, dt_bias, d_skip, q_norm, k_norm, ssm_norm, w_out,
           mix_norm_post, ff2_norm_pre, ff2_w_gate, ff2_w_up, ff2_w_down, ff2_norm_post):
    assert ff1_w_gate.shape[0] == 1, "single-layer trunk"
    row2 = lambda v: v.reshape(1, -1).astype(F32)
    ff1 = (row2(ff1_norm_pre[0]), ff1_w_gate[0].astype(BF16), ff1_w_up[0].astype(BF16), ff1_w_down[0].astype(BF16),
           row2(ff1_norm_post[0]))
    ff2 = (row2(ff2_norm_pre[0]), ff2_w_gate[0].astype(BF16), ff2_w_up[0].astype(BF16), ff2_w_down[0].astype(BF16),
           row2(ff2_norm_post[0]))
    w_in0 = w_in[0]
    wqkv_t = w_in0[:, :QKV_WIDTH].T.astype(BF16)
    dt_cols = w_in0[:, QKV_WIDTH + SSM_WIDTH + CONV_DIM:]
    wssm = jnp.concatenate([w_in0[:, QKV_WIDTH:QKV_WIDTH + SSM_WIDTH + CONV_DIM], dt_cols,
                            jnp.zeros((D_MODEL, DT_LANES - dt_cols.shape[1]), F32)], axis=1).astype(BF16)
    g_mix = row2(mix_norm_pre[0])
    gq = q_norm[0].reshape(HEAD_DIM, 1).astype(F32)
    gk = k_norm[0].reshape(HEAD_DIM, 1).astype(F32)
    n_dt = 2 * N_SSM_HEADS
    dtb = jnp.zeros((1, DT_LANES), F32).at[0, :n_dt].set(dt_bias[0].reshape(-1))
    alog = jnp.zeros((1, DT_LANES), F32).at[0, n_dt:2 * n_dt].set(a_log[0].reshape(-1))
    dskip = jnp.repeat(d_skip[0].astype(F32), HEAD_DIM).reshape(1, SSM_WIDTH)
    gnorm = row2(ssm_norm[0])
    cw = conv_w[0].astype(F32)
    cb = row2(conv_b[0])
    w_out_att = w_out[0][:ATT_WIDTH].astype(BF16)
    w_out_ssm = w_out[0][ATT_WIDTH:].astype(BF16)
    g_post = row2(mix_norm_post[0])

    def in_projection(h, cos_t, sin_t, mask, tm):
        return _inproj(h, g_mix, wqkv_t, wssm, cos_t, sin_t, gq, gk, mask, tm=tm)

    x_meta = jnp.concatenate([jnp.zeros((META_LEAD, D_MODEL), F32), meta_tokens.astype(F32)], axis=0)
    h_meta = _ffn(x_meta, *ff1, tm=META_ROWS)
    mask_meta = jnp.where(jnp.arange(META_ROWS) < META_LEAD, MASK_NEG, 0.0).astype(F32).reshape(1, META_ROWS)
    _, k_meta, v_t_meta, _, xbc_meta, dt_meta = in_projection(
        h_meta, jnp.ones((HEAD_DIM, META_ROWS), F32), jnp.zeros((HEAD_DIM, META_ROWS), F32), mask_meta, META_ROWS)

    def trunk(x):
        batch, seq, _ = x.shape
        n = batch * seq
        tm = _tile_of(n, ROW_TILE)
        h1 = _ffn(x.reshape(n, D_MODEL), *ff1, tm=tm)
        cos_t, sin_t = _rope_tables_t(seq)
        tm_in = _tile_of(seq, ROW_TILE)
        q_t, k, v_t, z, xbc, dt_raw = in_projection(h1, cos_t, sin_t, jnp.zeros((1, seq), F32), tm_in)
        o_att = _attention(q_t, k, v_t, k_meta, v_t_meta, batch=batch, seq=seq,
                           tq=_tile_of(seq, ATT_TQ), tk=_tile_of(seq, ATT_TK))
        o_ssm = _ssd(xbc, dt_raw, z, xbc_meta, dt_meta, cw, cb, dtb, alog, dskip, gnorm, batch=batch, seq=seq)
        h2 = _outproj(o_att, o_ssm, h1, w_out_att, w_out_ssm, g_post, tm=tm)
        return _ffn(h2, *ff2, tm=tm).reshape(batch, seq, D_MODEL)

    return trunk(x_prompt), trunk(x_sample)
```

```python
import functools
import math

import jax
import jax.numpy as jnp
from jax import lax
from jax.experimental import pallas as pl
from jax.experimental.pallas import tpu as pltpu

F32 = jnp.float32
BF16 = jnp.bfloat16

D_MODEL = 2048
D_FF = 5632
N_META = 16
GRID_W = 64
EPS = 1e-6
HEAD_DIM = 64
ATT_WIDTH = 1024
N_Q_HEADS = 16
N_KV_HEADS = 4
Q_PER_KV = 4
ROPE_THETA = 10000.0
SSM_WIDTH = 1024
N_SSM_HEADS = 16
N_SSM_GROUPS = 2
HEADS_PER_GROUP = N_SSM_HEADS // N_SSM_GROUPS
D_STATE = 128
D_CONV = 5
CHUNK = 128
CONV_DIM = SSM_WIDTH + 2 * N_SSM_GROUPS * D_STATE
KV_WIDTH = N_KV_HEADS * HEAD_DIM
QKV_WIDTH = ATT_WIDTH + 2 * KV_WIDTH
DT_LANES = 128
SSM_PROJ = SSM_WIDTH + CONV_DIM + DT_LANES

META_ROWS = CHUNK
META_LEAD = CHUNK - N_META
QK_AUG = 128
V_AUG = 80
MASK_NEG = -1e30
CONV_HALO = 8

FF_TILE = 512
ROW_TILE = 512
ATT_TQ = 512
ATT_TK = 512
ATT_RING = 4
VMEM_LIMIT = 56 * 1024 * 1024


def _cparams(sem):
    return pltpu.CompilerParams(dimension_semantics=sem, vmem_limit_bytes=VMEM_LIMIT)


def _rms(x, gain):
    ms = jnp.mean(x * x, axis=-1, keepdims=True)
    return x * lax.rsqrt(ms + EPS) * gain


def _ffn_body(x_ref, gpre_ref, wg_ref, wu_ref, wd_ref, gpost_ref, o_ref, u_ref):
    j = pl.program_id(1)

    @pl.when(j == 0)
    def _():
        u_ref[...] = _rms(x_ref[...], gpre_ref[...]).astype(BF16)
        o_ref[...] = jnp.zeros_like(o_ref)

    u = u_ref[...]
    g = jnp.dot(u, wg_ref[...], preferred_element_type=F32)
    up = jnp.dot(u, wu_ref[...], preferred_element_type=F32)
    act = (g * jax.nn.sigmoid(g) * up).astype(BF16)
    o_ref[...] += jnp.dot(act, wd_ref[...], preferred_element_type=F32)

    @pl.when(j == pl.num_programs(1) - 1)
    def _():
        o_ref[...] = x_ref[...] + 0.5 * _rms(o_ref[...], gpost_ref[...])


def _ffn(x, gpre, wg, wu, wd, gpost, *, tm):
    n = x.shape[0]
    return pl.pallas_call(
        _ffn_body,
        grid=(n // tm, D_FF // FF_TILE),
        in_specs=[
            pl.BlockSpec((tm, D_MODEL), lambda i, j: (i, 0)),
            pl.BlockSpec((1, D_MODEL), lambda i, j: (0, 0)),
            pl.BlockSpec((D_MODEL, FF_TILE), lambda i, j: (0, j)),
            pl.BlockSpec((D_MODEL, FF_TILE), lambda i, j: (0, j)),
            pl.BlockSpec((FF_TILE, D_MODEL), lambda i, j: (j, 0)),
            pl.BlockSpec((1, D_MODEL), lambda i, j: (0, 0)),
        ],
        out_specs=pl.BlockSpec((tm, D_MODEL), lambda i, j: (i, 0)),
        out_shape=jax.ShapeDtypeStruct((n, D_MODEL), F32),
        scratch_shapes=[pltpu.VMEM((tm, D_MODEL), BF16)],
        compiler_params=_cparams(("parallel", "arbitrary")),
        name="ffn",
    )(x, gpre, wg, wu, wd, gpost)


def _inproj_body(h_ref, g_ref, wqkv_t_ref, wssm_ref, cos_ref, sin_ref, gq_ref, gk_ref, mask_ref,
                 q_t_ref, k_ref, v_t_ref, z_ref, xbc_ref, dt_ref):
    tm = h_ref.shape[0]
    u = _rms(h_ref[...], g_ref[...]).astype(BF16)
    p_t = lax.dot_general(wqkv_t_ref[...], u, (((1,), (1,)), ((), ())), preferred_element_type=F32)
    ps = jnp.dot(u, wssm_ref[...], preferred_element_type=F32)
    z_ref[...] = ps[:, :SSM_WIDTH]
    xbc_ref[...] = ps[:, SSM_WIDTH:SSM_WIDTH + CONV_DIM]
    dt_ref[...] = ps[:, SSM_WIDTH + CONV_DIM:]

    cos_t = cos_ref[...]
    sin_t = sin_ref[...]

    def norm_rope(x, gain):
        ss = jnp.sum(x * x, axis=0, keepdims=True)
        xg = x * lax.rsqrt(ss * (1.0 / HEAD_DIM) + EPS) * gain
        xs = jnp.concatenate([xg[16:32], xg[0:16], xg[48:64], xg[32:48]], axis=0)
        return xg * cos_t + xs * sin_t

    def first_row(rows, value):
        row = lax.broadcasted_iota(jnp.int32, (rows, tm), 0)
        return jnp.where(row == 0, value, 0.0)

    q_tail = first_row(QK_AUG - HEAD_DIM, 1.0).astype(BF16)
    k_tail = first_row(QK_AUG - HEAD_DIM, jnp.broadcast_to(mask_ref[...], (QK_AUG - HEAD_DIM, tm)))
    v_tail = first_row(V_AUG - HEAD_DIM, 1.0).astype(BF16)
    gq = gq_ref[...]
    gk = gk_ref[...]
    q_scale = HEAD_DIM ** -0.5 * math.log2(math.e)
    for h in range(N_Q_HEADS):
        q = norm_rope(p_t[HEAD_DIM * h:HEAD_DIM * (h + 1)], gq) * q_scale
        q_t_ref[h, 0:HEAD_DIM, :] = q.astype(BF16)
        q_t_ref[h, HEAD_DIM:QK_AUG, :] = q_tail
    for h in range(N_KV_HEADS):
        k = norm_rope(p_t[ATT_WIDTH + HEAD_DIM * h:ATT_WIDTH + HEAD_DIM * (h + 1)], gk)
        k_aug_t = jnp.concatenate([k, k_tail], axis=0)
        k_ref[h] = k_aug_t.T.astype(BF16)
        v0 = ATT_WIDTH + KV_WIDTH + HEAD_DIM * h
        v_t_ref[h, 0:HEAD_DIM, :] = p_t[v0:v0 + HEAD_DIM].astype(BF16)
        v_t_ref[h, HEAD_DIM:V_AUG, :] = v_tail


def _inproj(h, gain, wqkv_t, wssm, cos_t, sin_t, gq, gk, mask, *, tm):
    n = h.shape[0]
    n_tab = cos_t.shape[1] // tm
    const = lambda i: (0, 0)
    return pl.pallas_call(
        _inproj_body,
        grid=(n // tm,),
        in_specs=[
            pl.BlockSpec((tm, D_MODEL), lambda i: (i, 0)),
            pl.BlockSpec((1, D_MODEL), const),
            pl.BlockSpec((QKV_WIDTH, D_MODEL), const, pipeline_mode=pl.Buffered(1)),
            pl.BlockSpec((D_MODEL, SSM_PROJ), const, pipeline_mode=pl.Buffered(1)),
            pl.BlockSpec((HEAD_DIM, tm), lambda i: (0, i % n_tab)),
            pl.BlockSpec((HEAD_DIM, tm), lambda i: (0, i % n_tab)),
            pl.BlockSpec((HEAD_DIM, 1), const),
            pl.BlockSpec((HEAD_DIM, 1), const),
            pl.BlockSpec((1, tm), lambda i: (0, i % n_tab)),
        ],
        out_specs=[
            pl.BlockSpec((N_Q_HEADS, QK_AUG, tm), lambda i: (0, 0, i)),
            pl.BlockSpec((N_KV_HEADS, tm, QK_AUG), lambda i: (0, i, 0)),
            pl.BlockSpec((N_KV_HEADS, V_AUG, tm), lambda i: (0, 0, i)),
            pl.BlockSpec((tm, SSM_WIDTH), lambda i: (i, 0)),
            pl.BlockSpec((tm, CONV_DIM), lambda i: (i, 0)),
            pl.BlockSpec((tm, DT_LANES), lambda i: (i, 0)),
        ],
        out_shape=[
            jax.ShapeDtypeStruct((N_Q_HEADS, QK_AUG, n), BF16),
            jax.ShapeDtypeStruct((N_KV_HEADS, n, QK_AUG), BF16),
            jax.ShapeDtypeStruct((N_KV_HEADS, V_AUG, n), BF16),
            jax.ShapeDtypeStruct((n, SSM_WIDTH), F32),
            jax.ShapeDtypeStruct((n, CONV_DIM), F32),
            jax.ShapeDtypeStruct((n, DT_LANES), F32),
        ],
        compiler_params=_cparams(("parallel",)),
        name="in_proj",
    )(h, gain, wqkv_t, wssm, cos_t, sin_t, gq, gk, mask)


def _attn_body(zero_ref, q_t_ref, k_ref, v_t_ref, km_ref, vm_ref, o_ref, m_ref, acc_ref, *ring):
    kj = pl.program_id(2)
    s_bufs, ml_bufs = ring[:ATT_RING], ring[ATT_RING:]
    rows = pl.ds(pl.multiple_of(zero_ref[0], 8), k_ref.shape[1])

    def scores(h, s_ref, ml_ref):
        s = jnp.dot(k_ref[h // Q_PER_KV], q_t_ref[h], preferred_element_type=F32)
        s_ref[rows, :] = s
        ml_ref[...] = jnp.max(s, axis=0, keepdims=True)

    def softmax_pv(h, s_ref, ml_ref):
        m_old = m_ref[h]
        m_new = jnp.maximum(m_old, ml_ref[...])
        p = jnp.exp2(s_ref[rows, :] - m_new).astype(BF16)
        pv = jnp.dot(v_t_ref[h // Q_PER_KV], p, preferred_element_type=F32)
        acc_ref[h] = acc_ref[h] * jnp.exp2(m_old - m_new) + pv
        m_ref[h] = m_new

    @pl.when(kj == 0)
    def _():
        def meta_head(h, carry):
            s = jnp.dot(km_ref[h // Q_PER_KV], q_t_ref[h], preferred_element_type=F32)
            m = jnp.max(s, axis=0, keepdims=True)
            p = jnp.exp2(s - m).astype(BF16)
            acc_ref[h] = jnp.dot(vm_ref[h // Q_PER_KV], p, preferred_element_type=F32)
            m_ref[h] = m
            return carry

        lax.fori_loop(0, N_Q_HEADS, meta_head, 0)

    for h in range(N_Q_HEADS + ATT_RING - 1):
        if h < N_Q_HEADS:
            scores(h, s_bufs[h % ATT_RING], ml_bufs[h % ATT_RING])
        d = h - (ATT_RING - 1)
        if d >= 0:
            softmax_pv(d, s_bufs[d % ATT_RING], ml_bufs[d % ATT_RING])

    @pl.when(kj == pl.num_programs(2) - 1)
    def _():
        for g in range(N_KV_HEADS):
            outs = []
            for hh in range(Q_PER_KV):
                acc = acc_ref[g * Q_PER_KV + hh]
                outs.append(acc[0:HEAD_DIM] / acc[HEAD_DIM:HEAD_DIM + 1])
            w = Q_PER_KV * HEAD_DIM
            o_ref[:, g * w:(g + 1) * w] = jnp.concatenate(outs, axis=0).T.astype(BF16)


def _attention(q_t, k, v_t, k_meta, v_t_meta, *, batch, seq, tq, tk):
    n = batch * seq
    nq, nk = seq // tq, seq // tk
    grid_spec = pltpu.PrefetchScalarGridSpec(
        num_scalar_prefetch=1,
        grid=(batch, nq, nk),
        in_specs=[
            pl.BlockSpec((N_Q_HEADS, QK_AUG, tq), lambda b, i, j, z: (0, 0, b * nq + i)),
            pl.BlockSpec((N_KV_HEADS, tk, QK_AUG), lambda b, i, j, z: (0, b * nk + j, 0)),
            pl.BlockSpec((N_KV_HEADS, V_AUG, tk), lambda b, i, j, z: (0, 0, b * nk + j)),
            pl.BlockSpec((N_KV_HEADS, META_ROWS, QK_AUG), lambda b, i, j, z: (0, 0, 0)),
            pl.BlockSpec((N_KV_HEADS, V_AUG, META_ROWS), lambda b, i, j, z: (0, 0, 0)),
        ],
        out_specs=pl.BlockSpec((tq, ATT_WIDTH), lambda b, i, j, z: (b * nq + i, 0)),
        scratch_shapes=[
            pltpu.VMEM((N_Q_HEADS, 1, tq), F32),
            pltpu.VMEM((N_Q_HEADS, V_AUG, tq), F32),
        ] + [pltpu.VMEM((tk, tq), F32)] * ATT_RING + [pltpu.VMEM((1, tq), F32)] * ATT_RING,
    )
    return pl.pallas_call(
        _attn_body,
        grid_spec=grid_spec,
        out_shape=jax.ShapeDtypeStruct((n, ATT_WIDTH), BF16),
        compiler_params=_cparams(("parallel", "parallel", "arbitrary")),
        name="attention",
    )(jnp.zeros((1,), jnp.int32), q_t, k, v_t, k_meta, v_t_meta)


def _split3(x):
    hi = x.astype(BF16)
    r1 = x - hi.astype(F32)
    mid = r1.astype(BF16)
    lo = (r1 - mid.astype(F32)).astype(BF16)
    return hi, mid, lo


def _tri_dot(tri, x):
    hi, mid, lo = _split3(x)
    return (jnp.dot(tri, hi, preferred_element_type=F32) + jnp.dot(tri, mid, preferred_element_type=F32)
            + jnp.dot(tri, lo, preferred_element_type=F32))


def _tri_masks():
    r = lax.broadcasted_iota(jnp.int32, (CHUNK, CHUNK), 0)
    c = lax.broadcasted_iota(jnp.int32, (CHUNK, CHUNK), 1)
    return c <= r, c >= r


def _dt_comb(dt_raw, dtb_ref, alog_ref, valid_from=None):
    x = dt_raw + dtb_ref[...]
    sp = jnp.maximum(x, 0.0) + jnp.log(1.0 + jnp.exp(-jnp.abs(x)))
    lane = lax.broadcasted_iota(jnp.int32, sp.shape, 1)
    dtv = jnp.where(lane < 2 * N_SSM_HEADS, sp, 0.0)
    if valid_from is not None:
        row = lax.broadcasted_iota(jnp.int32, sp.shape, 0)
        dtv = jnp.where(row >= valid_from, dtv, 0.0)
    lane1 = lax.broadcasted_iota(jnp.int32, (1, DT_LANES), 1)
    a_mult = jnp.where((lane1 >= 2 * N_SSM_HEADS) & (lane1 < 4 * N_SSM_HEADS), -jnp.exp(alog_ref[...]), 0.0)
    return dtv + pltpu.roll(dtv, 2 * N_SSM_HEADS, axis=1) * a_mult


def _conv_silu(ext_ref, cw_ref, cb_ref):
    acc = cb_ref[...] + ext_ref[pl.ds(CONV_HALO - D_CONV // 2, CHUNK), :] * cw_ref[0:1, :]
    for j in range(1, D_CONV):
        acc = acc + ext_ref[pl.ds(CONV_HALO - D_CONV // 2 + j, CHUNK), :] * cw_ref[j:j + 1, :]
    return acc * jax.nn.sigmoid(acc)


DT_F, DT_B = 0, N_SSM_HEADS
A_F, A_B = 2 * N_SSM_HEADS, 3 * N_SSM_HEADS


def _chunk_states(xs_t, comb_t, cum_t, b_mat, dt_row0, a_row0, end_col):
    dec = jnp.exp(cum_t[a_row0:a_row0 + N_SSM_HEADS, end_col:end_col + 1] - cum_t[a_row0:a_row0 + N_SSM_HEADS])
    out = []
    for g in range(N_SSM_GROUPS):
        rows = []
        for hh in range(HEADS_PER_GROUP):
            h = g * HEADS_PER_GROUP + hh
            xr = xs_t[HEAD_DIM * h:HEAD_DIM * (h + 1)] * (comb_t[dt_row0 + h:dt_row0 + h + 1] * dec[h:h + 1])
            rows.append(xr.astype(BF16))
        xrd = jnp.concatenate(rows, axis=0)
        out.append(jnp.dot(xrd, b_mat[:, D_STATE * g:D_STATE * (g + 1)].astype(BF16), preferred_element_type=F32))
    return out


def _update_state(state_ref, contrib, cum_t, a_row0, end_col):
    cd = jnp.exp(cum_t[a_row0:a_row0 + N_SSM_HEADS, end_col:end_col + 1])
    for g in range(N_SSM_GROUPS):
        for hh in range(HEADS_PER_GROUP):
            h = g * HEADS_PER_GROUP + hh
            sl = slice(HEAD_DIM * h, HEAD_DIM * (h + 1))
            state_ref[sl, :] = state_ref[sl, :] * cd[h:h + 1] + contrib[g][HEAD_DIM * hh:HEAD_DIM * (hh + 1)]


def _ssd_fwd_body(xc_ref, xp_ref, xn_ref, dt_ref, xm_ref, dtm_ref, cw_ref, cb_ref, dtb_ref, alog_ref, dskip_ref,
                  ypart_ref, xact_ref, state_ref, ext_ref):
    t = pl.program_id(1)
    ltri, utri = _tri_masks()
    ltri_b = jnp.where(ltri, 1.0, 0.0).astype(BF16)
    utri_b = jnp.where(utri, 1.0, 0.0).astype(BF16)

    @pl.when(t == 0)
    def _():
        ext_ref[0:CONV_HALO, :] = jnp.zeros((CONV_HALO, CONV_DIM), F32)
        ext_ref[CONV_HALO:CONV_HALO + CHUNK, :] = xm_ref[...]
        ext_ref[CONV_HALO + CHUNK:, :] = xc_ref[0:CONV_HALO, :]
        xact = _conv_silu(ext_ref, cw_ref, cb_ref)
        comb = _dt_comb(dtm_ref[...], dtb_ref, alog_ref, valid_from=META_LEAD)
        pc_t = _tri_dot(ltri_b, comb).T
        contrib = _chunk_states(xact[:, :SSM_WIDTH].T, comb.T, pc_t, xact[:, SSM_WIDTH:SSM_WIDTH + N_SSM_GROUPS * D_STATE],
                                DT_F, A_F, CHUNK - 1)
        for g in range(N_SSM_GROUPS):
            state_ref[HEADS_PER_GROUP * HEAD_DIM * g:HEADS_PER_GROUP * HEAD_DIM * (g + 1), :] = contrib[g]

    ext_ref[0:CONV_HALO, :] = jnp.where(t == 0, xm_ref[CHUNK - CONV_HALO:, :], xp_ref[...])
    ext_ref[CONV_HALO:CONV_HALO + CHUNK, :] = xc_ref[...]
    ext_ref[CONV_HALO + CHUNK:, :] = jnp.where(t == pl.num_programs(1) - 1, 0.0, xn_ref[...])
    xact = _conv_silu(ext_ref, cw_ref, cb_ref)
    xact_ref[...] = xact
    xs = xact[:, :SSM_WIDTH]
    b_mat = xact[:, SSM_WIDTH:SSM_WIDTH + N_SSM_GROUPS * D_STATE]
    c_mat = xact[:, SSM_WIDTH + N_SSM_GROUPS * D_STATE:]

    comb = _dt_comb(dt_ref[...], dtb_ref, alog_ref)
    pc = _tri_dot(ltri_b, comb)
    sc = _tri_dot(utri_b, comb)
    pc_t, sc_t, comb_t = pc.T, sc.T, comb.T
    xs_t = xs.T
    exp_p = jnp.exp(pc_t[A_F:A_F + N_SSM_HEADS])

    y_rows = []
    for g in range(N_SSM_GROUPS):
        bg = b_mat[:, D_STATE * g:D_STATE * (g + 1)].astype(BF16)
        cg_t = c_mat[:, D_STATE * g:D_STATE * (g + 1)].T.astype(BF16)
        cb_t = jnp.dot(bg, cg_t, preferred_element_type=F32)
        cb_f = jnp.where(utri, cb_t, 0.0)
        cb_b = jnp.where(ltri, cb_t, 0.0)
        st = state_ref[HEADS_PER_GROUP * HEAD_DIM * g:HEADS_PER_GROUP * HEAD_DIM * (g + 1), :].astype(BF16)
        yo_t = jnp.dot(st, cg_t, preferred_element_type=F32)
        for hh in range(HEADS_PER_GROUP):
            h = g * HEADS_PER_GROUP + hh
            ef = jnp.exp(jnp.minimum(pc_t[A_F + h:A_F + h + 1] - pc[:, A_F + h:A_F + h + 1], 0.0))
            eb = jnp.exp(jnp.minimum(sc_t[A_B + h:A_B + h + 1] - sc[:, A_B + h:A_B + h + 1], 0.0))
            m_t = jnp.concatenate([(cb_f * ef).astype(BF16), (cb_b * eb).astype(BF16)], axis=0)
            x_h = xs_t[HEAD_DIM * h:HEAD_DIM * (h + 1)]
            xr = jnp.concatenate([(x_h * comb_t[DT_F + h:DT_F + h + 1]).astype(BF16),
                                  (x_h * comb_t[DT_B + h:DT_B + h + 1]).astype(BF16)], axis=1)
            y_h = jnp.dot(xr, m_t, preferred_element_type=F32)
            y_rows.append(y_h + yo_t[HEAD_DIM * hh:HEAD_DIM * (hh + 1)] * exp_p[h:h + 1])
    y_t = jnp.concatenate(y_rows, axis=0)
    ypart_ref[...] = y_t.T + xs * dskip_ref[...]

    contrib = _chunk_states(xs_t, comb_t, pc_t, b_mat, DT_F, A_F, CHUNK - 1)
    _update_state(state_ref, contrib, pc_t, A_F, CHUNK - 1)


def _ssd_bwd_body(xact_ref, dt_ref, ypart_ref, z_ref, dtb_ref, alog_ref, gnorm_ref, o_ref, state_ref):
    t = pl.program_id(1)
    _, utri = _tri_masks()
    utri_b = jnp.where(utri, 1.0, 0.0).astype(BF16)

    @pl.when(t == 0)
    def _():
        state_ref[...] = jnp.zeros_like(state_ref)

    xact = xact_ref[...]
    xs = xact[:, :SSM_WIDTH]
    b_mat = xact[:, SSM_WIDTH:SSM_WIDTH + N_SSM_GROUPS * D_STATE]
    c_mat = xact[:, SSM_WIDTH + N_SSM_GROUPS * D_STATE:]
    comb = _dt_comb(dt_ref[...], dtb_ref, alog_ref)
    sc_t = _tri_dot(utri_b, comb).T
    comb_t = comb.T
    exp_s = jnp.exp(sc_t[A_B:A_B + N_SSM_HEADS])

    y_rows = []
    for g in range(N_SSM_GROUPS):
        cg_t = c_mat[:, D_STATE * g:D_STATE * (g + 1)].T.astype(BF16)
        st = state_ref[HEADS_PER_GROUP * HEAD_DIM * g:HEADS_PER_GROUP * HEAD_DIM * (g + 1), :].astype(BF16)
        yo_t = jnp.dot(st, cg_t, preferred_element_type=F32)
        for hh in range(HEADS_PER_GROUP):
            h = g * HEADS_PER_GROUP + hh
            y_rows.append(yo_t[HEAD_DIM * hh:HEAD_DIM * (hh + 1)] * exp_s[h:h + 1])
    y = ypart_ref[...] + jnp.concatenate(y_rows, axis=0).T
    z = z_ref[...]
    y = y * (z * jax.nn.sigmoid(z))
    gw = SSM_WIDTH // N_SSM_GROUPS
    for g in range(N_SSM_GROUPS):
        o_ref[:, gw * g:gw * (g + 1)] = _rms(y[:, gw * g:gw * (g + 1)], gnorm_ref[:, gw * g:gw * (g + 1)]).astype(BF16)

    contrib = _chunk_states(xs.T, comb_t, sc_t, b_mat, DT_B, A_B, 0)
    _update_state(state_ref, contrib, sc_t, A_B, 0)


def _ssd(xbc, dt_raw, z, xbc_meta, dt_meta, conv_w, conv_b, dtb, alog, dskip, gnorm, *, batch, seq):
    n = batch * seq
    nc = seq // CHUNK
    hb = CHUNK // CONV_HALO
    const = lambda b, t: (0, 0)
    ypart, xact = pl.pallas_call(
        _ssd_fwd_body,
        grid=(batch, nc),
        in_specs=[
            pl.BlockSpec((CHUNK, CONV_DIM), lambda b, t: (b * nc + t, 0)),
            pl.BlockSpec((CONV_HALO, CONV_DIM), lambda b, t: (jnp.maximum((b * nc + t) * hb - 1, 0), 0)),
            pl.BlockSpec((CONV_HALO, CONV_DIM), lambda b, t: (jnp.minimum((b * nc + t + 1) * hb, n // CONV_HALO - 1), 0)),
            pl.BlockSpec((CHUNK, DT_LANES), lambda b, t: (b * nc + t, 0)),
            pl.BlockSpec((META_ROWS, CONV_DIM), const),
            pl.BlockSpec((META_ROWS, DT_LANES), const),
            pl.BlockSpec((D_CONV, CONV_DIM), const),
            pl.BlockSpec((1, CONV_DIM), const),
            pl.BlockSpec((1, DT_LANES), const),
            pl.BlockSpec((1, DT_LANES), const),
            pl.BlockSpec((1, SSM_WIDTH), const),
        ],
        out_specs=[
            pl.BlockSpec((CHUNK, SSM_WIDTH), lambda b, t: (b * nc + t, 0)),
            pl.BlockSpec((CHUNK, CONV_DIM), lambda b, t: (b * nc + t, 0)),
        ],
        out_shape=[
            jax.ShapeDtypeStruct((n, SSM_WIDTH), F32),
            jax.ShapeDtypeStruct((n, CONV_DIM), F32),
        ],
        scratch_shapes=[
            pltpu.VMEM((N_SSM_HEADS * HEAD_DIM, D_STATE), F32),
            pltpu.VMEM((CHUNK + 2 * CONV_HALO, CONV_DIM), F32),
        ],
        compiler_params=_cparams(("parallel", "arbitrary")),
        name="ssd_fwd",
    )(xbc, xbc, xbc, dt_raw, xbc_meta, dt_meta, conv_w, conv_b, dtb, alog, dskip)

    rev = lambda b, t: (b * nc + nc - 1 - t, 0)
    return pl.pallas_call(
        _ssd_bwd_body,
        grid=(batch, nc),
        in_specs=[
            pl.BlockSpec((CHUNK, CONV_DIM), rev),
            pl.BlockSpec((CHUNK, DT_LANES), rev),
            pl.BlockSpec((CHUNK, SSM_WIDTH), rev),
            pl.BlockSpec((CHUNK, SSM_WIDTH), rev),
            pl.BlockSpec((1, DT_LANES), const),
            pl.BlockSpec((1, DT_LANES), const),
            pl.BlockSpec((1, SSM_WIDTH), const),
        ],
        out_specs=pl.BlockSpec((CHUNK, SSM_WIDTH), rev),
        out_shape=jax.ShapeDtypeStruct((n, SSM_WIDTH), BF16),
        scratch_shapes=[pltpu.VMEM((N_SSM_HEADS * HEAD_DIM, D_STATE), F32)],
        compiler_params=_cparams(("parallel", "arbitrary")),
        name="ssd_bwd",
    )(xact, dt_raw, ypart, z, dtb, alog, gnorm)


def _outproj_body(oa_ref, os_ref, h_ref, wa_ref, ws_ref, g_ref, o_ref):
    mix = (jnp.dot(oa_ref[...], wa_ref[...], preferred_element_type=F32)
           + jnp.dot(os_ref[...], ws_ref[...], preferred_element_type=F32))
    o_ref[...] = h_ref[...] + _rms(mix, g_ref[...])


def _outproj(o_att, o_ssm, h, w_att, w_ssm, gain, *, tm):
    n = h.shape[0]
    const = lambda i: (0, 0)
    return pl.pallas_call(
        _outproj_body,
        grid=(n // tm,),
        in_specs=[
            pl.BlockSpec((tm, ATT_WIDTH), lambda i: (i, 0)),
            pl.BlockSpec((tm, SSM_WIDTH), lambda i: (i, 0)),
            pl.BlockSpec((tm, D_MODEL), lambda i: (i, 0)),
            pl.BlockSpec((ATT_WIDTH, D_MODEL), const),
            pl.BlockSpec((SSM_WIDTH, D_MODEL), const),
            pl.BlockSpec((1, D_MODEL), const),
        ],
        out_specs=pl.BlockSpec((tm, D_MODEL), lambda i: (i, 0)),
        out_shape=jax.ShapeDtypeStruct((n, D_MODEL), F32),
        compiler_params=_cparams(("parallel",)),
        name="out_proj",
    )(o_att, o_ssm, h, w_att, w_ssm, gain)


def _rope_tables_t(n_tok):
    pos = jnp.arange(n_tok)
    row = (pos // GRID_W).astype(F32)
    col = (pos % GRID_W).astype(F32)
    n_freq = HEAD_DIM // 4
    inv_freq = ROPE_THETA ** (-jnp.arange(n_freq, dtype=F32) / n_freq)
    ang_r = inv_freq[:, None] * row[None, :]
    ang_c = inv_freq[:, None] * col[None, :]
    ang = jnp.concatenate([ang_r, ang_r, ang_c, ang_c], axis=0)
    sign = jnp.concatenate([-jnp.ones((n_freq, 1), F32), jnp.ones((n_freq, 1), F32)] * 2, axis=0)
    return jnp.cos(ang), jnp.sin(ang) * sign


def _tile_of(n, pref):
    return pref if n % pref == 0 else n


def kernel(x_prompt, x_sample, meta_tokens, ff1_norm_pre, ff1_w_gate, ff1_w_up, ff1_w_down, ff1_norm_post,
           mix_norm_pre, w_in, conv_w, conv_b, a_log, dt_bias, d_skip, q_norm, k_norm, ssm_norm, w_out,
           mix_norm_post, ff2_norm_pre, ff2_w_gate, ff2_w_up, ff2_w_down, ff2_norm_post):
    assert ff1_w_gate.shape[0] == 1, "single-layer trunk"
    row2 = lambda v: v.reshape(1, -1).astype(F32)
    ff1 = (row2(ff1_norm_pre[0]), ff1_w_gate[0].astype(BF16), ff1_w_up[0].astype(BF16), ff1_w_down[0].astype(BF16),
           row2(ff1_norm_post[0]))
    ff2 = (row2(ff2_norm_pre[0]), ff2_w_gate[0].astype(BF16), ff2_w_up[0].astype(BF16), ff2_w_down[0].astype(BF16),
           row2(ff2_norm_post[0]))
    w_in0 = w_in[0]
    wqkv_t = w_in0[:, :QKV_WIDTH].T.astype(BF16)
    dt_cols = w_in0[:, QKV_WIDTH + SSM_WIDTH + CONV_DIM:]
    wssm = jnp.concatenate([w_in0[:, QKV_WIDTH:QKV_WIDTH + SSM_WIDTH + CONV_DIM], dt_cols,
                            jnp.zeros((D_MODEL, DT_LANES - dt_cols.shape[1]), F32)], axis=1).astype(BF16)
    g_mix = row2(mix_norm_pre[0])
    gq = q_norm[0].reshape(HEAD_DIM, 1).astype(F32)
    gk = k_norm[0].reshape(HEAD_DIM, 1).astype(F32)
    n_dt = 2 * N_SSM_HEADS
    dtb = jnp.zeros((1, DT_LANES), F32).at[0, :n_dt].set(dt_bias[0].reshape(-1))
    alog = jnp.zeros((1, DT_LANES), F32).at[0, n_dt:2 * n_dt].set(a_log[0].reshape(-1))
    dskip = jnp.repeat(d_skip[0].astype(F32), HEAD_DIM).reshape(1, SSM_WIDTH)
    gnorm = row2(ssm_norm[0])
    cw = conv_w[0].astype(F32)
    cb = row2(conv_b[0])
    w_out_att = w_out[0][:ATT_WIDTH].astype(BF16)
    w_out_ssm = w_out[0][ATT_WIDTH:].astype(BF16)
    g_post = row2(mix_norm_post[0])

    def in_projection(h, cos_t, sin_t, mask, tm):
        return _inproj(h, g_mix, wqkv_t, wssm, cos_t, sin_t, gq, gk, mask, tm=tm)

    x_meta = jnp.concatenate([jnp.zeros((META_LEAD, D_MODEL), F32), meta_tokens.astype(F32)], axis=0)
    h_meta = _ffn(x_meta, *ff1, tm=META_ROWS)
    mask_meta = jnp.where(jnp.arange(META_ROWS) < META_LEAD, MASK_NEG, 0.0).astype(F32).reshape(1, META_ROWS)
    _, k_meta, v_t_meta, _, xbc_meta, dt_meta = in_projection(
        h_meta, jnp.ones((HEAD_DIM, META_ROWS), F32), jnp.zeros((HEAD_DIM, META_ROWS), F32), mask_meta, META_ROWS)

    def trunk(x):
        batch, seq, _ = x.shape
        n = batch * seq
        tm = _tile_of(n, ROW_TILE)
        h1 = _ffn(x.reshape(n, D_MODEL), *ff1, tm=tm)
        cos_t, sin_t = _rope_tables_t(seq)
        tm_in = _tile_of(seq, ROW_TILE)
        q_t, k, v_t, z, xbc, dt_raw = in_projection(h1, cos_t, sin_t, jnp.zeros((1, seq), F32), tm_in)
        o_att = _attention(q_t, k, v_t, k_meta, v_t_meta, batch=batch, seq=seq,
                           tq=_tile_of(seq, ATT_TQ), tk=_tile_of(seq, ATT_TK))
        o_ssm = _ssd(xbc, dt_raw, z, xbc_meta, dt_meta, cw, cb, dtb, alog, dskip, gnorm, batch=batch, seq=seq)
        h2 = _outproj(o_att, o_ssm, h1, w_out_att, w_out_ssm, g_post, tm=tm)
        return _ffn(h2, *ff2, tm=tm).reshape(batch, seq, D_MODEL)

    return trunk(x_prompt), trunk(x_sample)
```

```python
import functools
import math

import jax
import jax.numpy as jnp
from jax import lax
from jax.experimental import pallas as pl
from jax.experimental.pallas import tpu as pltpu

F32 = jnp.float32
BF16 = jnp.bfloat16

D_MODEL = 2048
D_FF = 5632
N_META = 16
GRID_W = 64
EPS = 1e-6
HEAD_DIM = 64
ATT_WIDTH = 1024
N_Q_HEADS = 16
N_KV_HEADS = 4
Q_PER_KV = 4
ROPE_THETA = 10000.0
SSM_WIDTH = 1024
N_SSM_HEADS = 16
N_SSM_GROUPS = 2
HEADS_PER_GROUP = N_SSM_HEADS // N_SSM_GROUPS
D_STATE = 128
D_CONV = 5
CHUNK = 128
CONV_DIM = SSM_WIDTH + 2 * N_SSM_GROUPS * D_STATE
KV_WIDTH = N_KV_HEADS * HEAD_DIM
QKV_WIDTH = ATT_WIDTH + 2 * KV_WIDTH
DT_LANES = 128
SSM_PROJ = SSM_WIDTH + CONV_DIM + DT_LANES

META_ROWS = CHUNK
META_LEAD = CHUNK - N_META
QK_AUG = 128
V_AUG = 80
CONV_HALO = 8

FF_TILE = 512
ROW_TILE = 512
ATT_TQ = 512
ATT_TK = 512
ATT_RING = 4
VMEM_LIMIT = 56 * 1024 * 1024


def _cparams(sem):
    return pltpu.CompilerParams(dimension_semantics=sem, vmem_limit_bytes=VMEM_LIMIT)


def _rms(x, gain):
    ms = jnp.mean(x * x, axis=-1, keepdims=True)
    return x * lax.rsqrt(ms + EPS) * gain


def _ffn_body(x_ref, gpre_ref, wg_ref, wu_ref, wd_ref, gpost_ref, o_ref, u_ref):
    j = pl.program_id(1)

    @pl.when(j == 0)
    def _():
        u_ref[...] = _rms(x_ref[...], gpre_ref[...]).astype(BF16)
        o_ref[...] = jnp.zeros_like(o_ref)

    u = u_ref[...]
    g = jnp.dot(u, wg_ref[...], preferred_element_type=F32)
    up = jnp.dot(u, wu_ref[...], preferred_element_type=F32)
    act = (g * jax.nn.sigmoid(g) * up).astype(BF16)
    o_ref[...] += jnp.dot(act, wd_ref[...], preferred_element_type=F32)

    @pl.when(j == pl.num_programs(1) - 1)
    def _():
        o_ref[...] = x_ref[...] + 0.5 * _rms(o_ref[...], gpost_ref[...])


def _ffn(x, gpre, wg, wu, wd, gpost, *, tm):
    n = x.shape[0]
    return pl.pallas_call(
        _ffn_body,
        grid=(n // tm, D_FF // FF_TILE),
        in_specs=[
            pl.BlockSpec((tm, D_MODEL), lambda i, j: (i, 0)),
            pl.BlockSpec((1, D_MODEL), lambda i, j: (0, 0)),
            pl.BlockSpec((D_MODEL, FF_TILE), lambda i, j: (0, j)),
            pl.BlockSpec((D_MODEL, FF_TILE), lambda i, j: (0, j)),
            pl.BlockSpec((FF_TILE, D_MODEL), lambda i, j: (j, 0)),
            pl.BlockSpec((1, D_MODEL), lambda i, j: (0, 0)),
        ],
        out_specs=pl.BlockSpec((tm, D_MODEL), lambda i, j: (i, 0)),
        out_shape=jax.ShapeDtypeStruct((n, D_MODEL), F32),
        scratch_shapes=[pltpu.VMEM((tm, D_MODEL), BF16)],
        compiler_params=_cparams(("parallel", "arbitrary")),
        name="ffn",
    )(x, gpre, wg, wu, wd, gpost)


def _inproj_body(h_ref, g_ref, wqkv_t_ref, wssm_ref, cos_ref, sin_ref, gq_ref, gk_ref,
                 q_t_ref, k_ref, v_t_ref, z_ref, xbc_ref, dt_ref):
    tm = h_ref.shape[0]
    u = _rms(h_ref[...], g_ref[...]).astype(BF16)
    p_t = lax.dot_general(wqkv_t_ref[...], u, (((1,), (1,)), ((), ())), preferred_element_type=F32)
    ps = jnp.dot(u, wssm_ref[...], preferred_element_type=F32)
    z_ref[...] = ps[:, :SSM_WIDTH]
    xbc_ref[...] = ps[:, SSM_WIDTH:SSM_WIDTH + CONV_DIM]
    dt_ref[...] = ps[:, SSM_WIDTH + CONV_DIM:]

    cos_t = cos_ref[...]
    sin_t = sin_ref[...]

    def norm_rope(x, gain):
        ss = jnp.sum(x * x, axis=0, keepdims=True)
        xg = x * lax.rsqrt(ss * (1.0 / HEAD_DIM) + EPS) * gain
        xs = jnp.concatenate([xg[16:32], xg[0:16], xg[48:64], xg[32:48]], axis=0)
        return xg * cos_t + xs * sin_t

    q_tail = jnp.zeros((QK_AUG - HEAD_DIM, tm), BF16)
    k_tail = jnp.zeros((QK_AUG - HEAD_DIM, tm), F32)
    v_row = lax.broadcasted_iota(jnp.int32, (V_AUG - HEAD_DIM, tm), 0)
    v_tail = jnp.where(v_row == 0, 1.0, 0.0).astype(BF16)
    gq = gq_ref[...]
    gk = gk_ref[...]
    q_scale = HEAD_DIM ** -0.5 * math.log2(math.e)
    for h in range(N_Q_HEADS):
        q = norm_rope(p_t[HEAD_DIM * h:HEAD_DIM * (h + 1)], gq) * q_scale
        q_t_ref[h, 0:HEAD_DIM, :] = q.astype(BF16)
        q_t_ref[h, HEAD_DIM:QK_AUG, :] = q_tail
    for h in range(N_KV_HEADS):
        k = norm_rope(p_t[ATT_WIDTH + HEAD_DIM * h:ATT_WIDTH + HEAD_DIM * (h + 1)], gk)
        k_aug_t = jnp.concatenate([k, k_tail], axis=0)
        k_ref[h] = k_aug_t.T.astype(BF16)
        v0 = ATT_WIDTH + KV_WIDTH + HEAD_DIM * h
        v_t_ref[h, 0:HEAD_DIM, :] = p_t[v0:v0 + HEAD_DIM].astype(BF16)
        v_t_ref[h, HEAD_DIM:V_AUG, :] = v_tail


def _inproj(h, gain, wqkv_t, wssm, cos_t, sin_t, gq, gk, *, tm):
    n = h.shape[0]
    n_tab = cos_t.shape[1] // tm
    const = lambda i: (0, 0)
    return pl.pallas_call(
        _inproj_body,
        grid=(n // tm,),
        in_specs=[
            pl.BlockSpec((tm, D_MODEL), lambda i: (i, 0)),
            pl.BlockSpec((1, D_MODEL), const),
            pl.BlockSpec((QKV_WIDTH, D_MODEL), const, pipeline_mode=pl.Buffered(1)),
            pl.BlockSpec((D_MODEL, SSM_PROJ), const, pipeline_mode=pl.Buffered(1)),
            pl.BlockSpec((HEAD_DIM, tm), lambda i: (0, i % n_tab)),
            pl.BlockSpec((HEAD_DIM, tm), lambda i: (0, i % n_tab)),
            pl.BlockSpec((HEAD_DIM, 1), const),
            pl.BlockSpec((HEAD_DIM, 1), const),
        ],
        out_specs=[
            pl.BlockSpec((N_Q_HEADS, QK_AUG, tm), lambda i: (0, 0, i)),
            pl.BlockSpec((N_KV_HEADS, tm, QK_AUG), lambda i: (0, i, 0)),
            pl.BlockSpec((N_KV_HEADS, V_AUG, tm), lambda i: (0, 0, i)),
            pl.BlockSpec((tm, SSM_WIDTH), lambda i: (i, 0)),
            pl.BlockSpec((tm, CONV_DIM), lambda i: (i, 0)),
            pl.BlockSpec((tm, DT_LANES), lambda i: (i, 0)),
        ],
        out_shape=[
            jax.ShapeDtypeStruct((N_Q_HEADS, QK_AUG, n), BF16),
            jax.ShapeDtypeStruct((N_KV_HEADS, n, QK_AUG), BF16),
            jax.ShapeDtypeStruct((N_KV_HEADS, V_AUG, n), BF16),
            jax.ShapeDtypeStruct((n, SSM_WIDTH), F32),
            jax.ShapeDtypeStruct((n, CONV_DIM), F32),
            jax.ShapeDtypeStruct((n, DT_LANES), F32),
        ],
        compiler_params=_cparams(("parallel",)),
        name="in_proj",
    )(h, gain, wqkv_t, wssm, cos_t, sin_t, gq, gk)


def _attn_body(zero_ref, q_t_ref, k_ref, v_t_ref, km_ref, vm_ref, o_ref, m_ref, acc_ref, *ring):
    kj = pl.program_id(2)
    s_bufs, ml_bufs = ring[:ATT_RING], ring[ATT_RING:]
    rows = pl.ds(pl.multiple_of(zero_ref[0], 8), k_ref.shape[1])

    def scores(h, s_ref, ml_ref):
        s = jnp.dot(k_ref[h // Q_PER_KV], q_t_ref[h], preferred_element_type=F32)
        s_ref[rows, :] = s
        ml_ref[...] = jnp.max(s, axis=0, keepdims=True)

    def softmax_pv(h, s_ref, ml_ref):
        m_old = m_ref[h]
        m_new = jnp.maximum(m_old, ml_ref[...])
        p = jnp.exp2(s_ref[rows, :] - m_new).astype(BF16)
        pv = jnp.dot(v_t_ref[h // Q_PER_KV], p, preferred_element_type=F32)
        acc_ref[h] = acc_ref[h] * jnp.exp2(m_old - m_new) + pv
        m_ref[h] = m_new

    @pl.when(kj == 0)
    def _():
        tq = q_t_ref.shape[2]
        p_pad = jnp.zeros((META_ROWS - N_META, tq), BF16)
        s_all = [jnp.dot(km_ref[h // Q_PER_KV], q_t_ref[h], preferred_element_type=F32) for h in range(N_Q_HEADS)]
        for h, s in enumerate(s_all):
            m = jnp.max(s, axis=0, keepdims=True)
            p = jnp.concatenate([jnp.exp2(s - m).astype(BF16), p_pad], axis=0)
            acc_ref[h] = jnp.dot(vm_ref[h // Q_PER_KV], p, preferred_element_type=F32)
            m_ref[h] = m

    for h in range(N_Q_HEADS + ATT_RING - 1):
        if h < N_Q_HEADS:
            scores(h, s_bufs[h % ATT_RING], ml_bufs[h % ATT_RING])
        d = h - (ATT_RING - 1)
        if d >= 0:
            softmax_pv(d, s_bufs[d % ATT_RING], ml_bufs[d % ATT_RING])

    @pl.when(kj == pl.num_programs(2) - 1)
    def _():
        for g in range(N_KV_HEADS):
            outs = []
            for hh in range(Q_PER_KV):
                acc = acc_ref[g * Q_PER_KV + hh]
                outs.append(acc[0:HEAD_DIM] / acc[HEAD_DIM:HEAD_DIM + 1])
            w = Q_PER_KV * HEAD_DIM
            o_ref[:, g * w:(g + 1) * w] = jnp.concatenate(outs, axis=0).T.astype(BF16)


def _attention(q_t, k, v_t, k_meta, v_t_meta, *, batch, seq, tq, tk):
    n = batch * seq
    nq, nk = seq // tq, seq // tk
    grid_spec = pltpu.PrefetchScalarGridSpec(
        num_scalar_prefetch=1,
        grid=(batch, nq, nk),
        in_specs=[
            pl.BlockSpec((N_Q_HEADS, QK_AUG, tq), lambda b, i, j, z: (0, 0, b * nq + i)),
            pl.BlockSpec((N_KV_HEADS, tk, QK_AUG), lambda b, i, j, z: (0, b * nk + j, 0)),
            pl.BlockSpec((N_KV_HEADS, V_AUG, tk), lambda b, i, j, z: (0, 0, b * nk + j)),
            pl.BlockSpec((N_KV_HEADS, N_META, QK_AUG), lambda b, i, j, z: (0, 0, 0)),
            pl.BlockSpec((N_KV_HEADS, V_AUG, META_ROWS), lambda b, i, j, z: (0, 0, 0)),
        ],
        out_specs=pl.BlockSpec((tq, ATT_WIDTH), lambda b, i, j, z: (b * nq + i, 0)),
        scratch_shapes=[
            pltpu.VMEM((N_Q_HEADS, 1, tq), F32),
            pltpu.VMEM((N_Q_HEADS, V_AUG, tq), F32),
        ] + [pltpu.VMEM((tk, tq), F32)] * ATT_RING + [pltpu.VMEM((1, tq), F32)] * ATT_RING,
    )
    return pl.pallas_call(
        _attn_body,
        grid_spec=grid_spec,
        out_shape=jax.ShapeDtypeStruct((n, ATT_WIDTH), BF16),
        compiler_params=_cparams(("parallel", "parallel", "arbitrary")),
        name="attention",
    )(jnp.zeros((1,), jnp.int32), q_t, k, v_t, k_meta, v_t_meta)


def _split3(x):
    hi = x.astype(BF16)
    r1 = x - hi.astype(F32)
    mid = r1.astype(BF16)
    lo = (r1 - mid.astype(F32)).astype(BF16)
    return hi, mid, lo


def _tri_dot(tri, x):
    hi, mid, lo = _split3(x)
    return (jnp.dot(tri, hi, preferred_element_type=F32) + jnp.dot(tri, mid, preferred_element_type=F32)
            + jnp.dot(tri, lo, preferred_element_type=F32))


def _tri_masks():
    r = lax.broadcasted_iota(jnp.int32, (CHUNK, CHUNK), 0)
    c = lax.broadcasted_iota(jnp.int32, (CHUNK, CHUNK), 1)
    return c <= r, c >= r


def _dt_comb(dt_raw, dtb_ref, alog_ref, valid_from=None):
    x = dt_raw + dtb_ref[...]
    sp = jnp.maximum(x, 0.0) + jnp.log(1.0 + jnp.exp(-jnp.abs(x)))
    lane = lax.broadcasted_iota(jnp.int32, sp.shape, 1)
    dtv = jnp.where(lane < 2 * N_SSM_HEADS, sp, 0.0)
    if valid_from is not None:
        row = lax.broadcasted_iota(jnp.int32, sp.shape, 0)
        dtv = jnp.where(row >= valid_from, dtv, 0.0)
    lane1 = lax.broadcasted_iota(jnp.int32, (1, DT_LANES), 1)
    a_mult = jnp.where((lane1 >= 2 * N_SSM_HEADS) & (lane1 < 4 * N_SSM_HEADS), -jnp.exp(alog_ref[...]), 0.0)
    return dtv + pltpu.roll(dtv, 2 * N_SSM_HEADS, axis=1) * a_mult


def _conv_silu(ext_ref, cw_ref, cb_ref):
    acc = cb_ref[...] + ext_ref[pl.ds(CONV_HALO - D_CONV // 2, CHUNK), :] * cw_ref[0:1, :]
    for j in range(1, D_CONV):
        acc = acc + ext_ref[pl.ds(CONV_HALO - D_CONV // 2 + j, CHUNK), :] * cw_ref[j:j + 1, :]
    return acc * jax.nn.sigmoid(acc)


DT_F, DT_B = 0, N_SSM_HEADS
A_F, A_B = 2 * N_SSM_HEADS, 3 * N_SSM_HEADS


def _chunk_states(xs_t, comb_t, cum_t, b_mat, dt_row0, a_row0, end_col):
    dec = jnp.exp(cum_t[a_row0:a_row0 + N_SSM_HEADS, end_col:end_col + 1] - cum_t[a_row0:a_row0 + N_SSM_HEADS])
    out = []
    for g in range(N_SSM_GROUPS):
        rows = []
        for hh in range(HEADS_PER_GROUP):
            h = g * HEADS_PER_GROUP + hh
            xr = xs_t[HEAD_DIM * h:HEAD_DIM * (h + 1)] * (comb_t[dt_row0 + h:dt_row0 + h + 1] * dec[h:h + 1])
            rows.append(xr.astype(BF16))
        xrd = jnp.concatenate(rows, axis=0)
        out.append(jnp.dot(xrd, b_mat[:, D_STATE * g:D_STATE * (g + 1)].astype(BF16), preferred_element_type=F32))
    return out


def _update_state(state_ref, contrib, cum_t, a_row0, end_col):
    cd = jnp.exp(cum_t[a_row0:a_row0 + N_SSM_HEADS, end_col:end_col + 1])
    for g in range(N_SSM_GROUPS):
        for hh in range(HEADS_PER_GROUP):
            h = g * HEADS_PER_GROUP + hh
            sl = slice(HEAD_DIM * h, HEAD_DIM * (h + 1))
            state_ref[sl, :] = state_ref[sl, :] * cd[h:h + 1] + contrib[g][HEAD_DIM * hh:HEAD_DIM * (hh + 1)]


def _ssd_fwd_body(xc_ref, xp_ref, xn_ref, dt_ref, xm_ref, dtm_ref, cw_ref, cb_ref, dtb_ref, alog_ref, dskip_ref,
                  ypart_ref, xact_ref, state_ref, ext_ref):
    t = pl.program_id(1)
    ltri, utri = _tri_masks()
    ltri_b = jnp.where(ltri, 1.0, 0.0).astype(BF16)
    utri_b = jnp.where(utri, 1.0, 0.0).astype(BF16)

    @pl.when(t == 0)
    def _():
        ext_ref[0:CONV_HALO, :] = jnp.zeros((CONV_HALO, CONV_DIM), F32)
        ext_ref[CONV_HALO:CONV_HALO + CHUNK, :] = xm_ref[...]
        ext_ref[CONV_HALO + CHUNK:, :] = xc_ref[0:CONV_HALO, :]
        xact = _conv_silu(ext_ref, cw_ref, cb_ref)
        comb = _dt_comb(dtm_ref[...], dtb_ref, alog_ref, valid_from=META_LEAD)
        pc_t = _tri_dot(ltri_b, comb).T
        contrib = _chunk_states(xact[:, :SSM_WIDTH].T, comb.T, pc_t, xact[:, SSM_WIDTH:SSM_WIDTH + N_SSM_GROUPS * D_STATE],
                                DT_F, A_F, CHUNK - 1)
        for g in range(N_SSM_GROUPS):
            state_ref[HEADS_PER_GROUP * HEAD_DIM * g:HEADS_PER_GROUP * HEAD_DIM * (g + 1), :] = contrib[g]

    ext_ref[0:CONV_HALO, :] = jnp.where(t == 0, xm_ref[CHUNK - CONV_HALO:, :], xp_ref[...])
    ext_ref[CONV_HALO:CONV_HALO + CHUNK, :] = xc_ref[...]
    ext_ref[CONV_HALO + CHUNK:, :] = jnp.where(t == pl.num_programs(1) - 1, 0.0, xn_ref[...])
    xact = _conv_silu(ext_ref, cw_ref, cb_ref)
    xact_ref[...] = xact
    xs = xact[:, :SSM_WIDTH]
    b_mat = xact[:, SSM_WIDTH:SSM_WIDTH + N_SSM_GROUPS * D_STATE]
    c_mat = xact[:, SSM_WIDTH + N_SSM_GROUPS * D_STATE:]

    comb = _dt_comb(dt_ref[...], dtb_ref, alog_ref)
    pc = _tri_dot(ltri_b, comb)
    sc = _tri_dot(utri_b, comb)
    pc_t, sc_t, comb_t = pc.T, sc.T, comb.T
    xs_t = xs.T
    exp_p = jnp.exp(pc_t[A_F:A_F + N_SSM_HEADS])

    y_rows = []
    for g in range(N_SSM_GROUPS):
        bg = b_mat[:, D_STATE * g:D_STATE * (g + 1)].astype(BF16)
        cg_t = c_mat[:, D_STATE * g:D_STATE * (g + 1)].T.astype(BF16)
        cb_t = jnp.dot(bg, cg_t, preferred_element_type=F32)
        cb_f = jnp.where(utri, cb_t, 0.0)
        cb_b = jnp.where(ltri, cb_t, 0.0)
        st = state_ref[HEADS_PER_GROUP * HEAD_DIM * g:HEADS_PER_GROUP * HEAD_DIM * (g + 1), :].astype(BF16)
        yo_t = jnp.dot(st, cg_t, preferred_element_type=F32)
        for hh in range(HEADS_PER_GROUP):
            h = g * HEADS_PER_GROUP + hh
            ef = jnp.exp(jnp.minimum(pc_t[A_F + h:A_F + h + 1] - pc[:, A_F + h:A_F + h + 1], 0.0))
            eb = jnp.exp(jnp.minimum(sc_t[A_B + h:A_B + h + 1] - sc[:, A_B + h:A_B + h + 1], 0.0))
            m_t = jnp.concatenate([(cb_f * ef).astype(BF16), (cb_b * eb).astype(BF16)], axis=0)
            x_h = xs_t[HEAD_DIM * h:HEAD_DIM * (h + 1)]
            xr = jnp.concatenate([(x_h * comb_t[DT_F + h:DT_F + h + 1]).astype(BF16),
                                  (x_h * comb_t[DT_B + h:DT_B + h + 1]).astype(BF16)], axis=1)
            y_h = jnp.dot(xr, m_t, preferred_element_type=F32)
            y_rows.append(y_h + yo_t[HEAD_DIM * hh:HEAD_DIM * (hh + 1)] * exp_p[h:h + 1])
    y_t = jnp.concatenate(y_rows, axis=0)
    ypart_ref[...] = y_t.T + xs * dskip_ref[...]

    contrib = _chunk_states(xs_t, comb_t, pc_t, b_mat, DT_F, A_F, CHUNK - 1)
    _update_state(state_ref, contrib, pc_t, A_F, CHUNK - 1)


def _ssd_bwd_body(xact_ref, dt_ref, ypart_ref, z_ref, dtb_ref, alog_ref, gnorm_ref, o_ref, state_ref):
    t = pl.program_id(1)
    _, utri = _tri_masks()
    utri_b = jnp.where(utri, 1.0, 0.0).astype(BF16)

    @pl.when(t == 0)
    def _():
        state_ref[...] = jnp.zeros_like(state_ref)

    xact = xact_ref[...]
    xs = xact[:, :SSM_WIDTH]
    b_mat = xact[:, SSM_WIDTH:SSM_WIDTH + N_SSM_GROUPS * D_STATE]
    c_mat = xact[:, SSM_WIDTH + N_SSM_GROUPS * D_STATE:]
    comb = _dt_comb(dt_ref[...], dtb_ref, alog_ref)
    sc_t = _tri_dot(utri_b, comb).T
    comb_t = comb.T
    exp_s = jnp.exp(sc_t[A_B:A_B + N_SSM_HEADS])

    y_rows = []
    for g in range(N_SSM_GROUPS):
        cg_t = c_mat[:, D_STATE * g:D_STATE * (g + 1)].T.astype(BF16)
        st = state_ref[HEADS_PER_GROUP * HEAD_DIM * g:HEADS_PER_GROUP * HEAD_DIM * (g + 1), :].astype(BF16)
        yo_t = jnp.dot(st, cg_t, preferred_element_type=F32)
        for hh in range(HEADS_PER_GROUP):
            h = g * HEADS_PER_GROUP + hh
            y_rows.append(yo_t[HEAD_DIM * hh:HEAD_DIM * (hh + 1)] * exp_s[h:h + 1])
    y = ypart_ref[...] + jnp.concatenate(y_rows, axis=0).T
    z = z_ref[...]
    y = y * (z * jax.nn.sigmoid(z))
    gw = SSM_WIDTH // N_SSM_GROUPS
    for g in range(N_SSM_GROUPS):
        o_ref[:, gw * g:gw * (g + 1)] = _rms(y[:, gw * g:gw * (g + 1)], gnorm_ref[:, gw * g:gw * (g + 1)]).astype(BF16)

    contrib = _chunk_states(xs.T, comb_t, sc_t, b_mat, DT_B, A_B, 0)
    _update_state(state_ref, contrib, sc_t, A_B, 0)


def _ssd(xbc, dt_raw, z, xbc_meta, dt_meta, conv_w, conv_b, dtb, alog, dskip, gnorm, *, batch, seq):
    n = batch * seq
    nc = seq // CHUNK
    hb = CHUNK // CONV_HALO
    const = lambda b, t: (0, 0)
    ypart, xact = pl.pallas_call(
        _ssd_fwd_body,
        grid=(batch, nc),
        in_specs=[
            pl.BlockSpec((CHUNK, CONV_DIM), lambda b, t: (b * nc + t, 0)),
            pl.BlockSpec((CONV_HALO, CONV_DIM), lambda b, t: (jnp.maximum((b * nc + t) * hb - 1, 0), 0)),
            pl.BlockSpec((CONV_HALO, CONV_DIM), lambda b, t: (jnp.minimum((b * nc + t + 1) * hb, n // CONV_HALO - 1), 0)),
            pl.BlockSpec((CHUNK, DT_LANES), lambda b, t: (b * nc + t, 0)),
            pl.BlockSpec((META_ROWS, CONV_DIM), const),
            pl.BlockSpec((META_ROWS, DT_LANES), const),
            pl.BlockSpec((D_CONV, CONV_DIM), const),
            pl.BlockSpec((1, CONV_DIM), const),
            pl.BlockSpec((1, DT_LANES), const),
            pl.BlockSpec((1, DT_LANES), const),
            pl.BlockSpec((1, SSM_WIDTH), const),
        ],
        out_specs=[
            pl.BlockSpec((CHUNK, SSM_WIDTH), lambda b, t: (b * nc + t, 0)),
            pl.BlockSpec((CHUNK, CONV_DIM), lambda b, t: (b * nc + t, 0)),
        ],
        out_shape=[
            jax.ShapeDtypeStruct((n, SSM_WIDTH), F32),
            jax.ShapeDtypeStruct((n, CONV_DIM), F32),
        ],
        scratch_shapes=[
            pltpu.VMEM((N_SSM_HEADS * HEAD_DIM, D_STATE), F32),
            pltpu.VMEM((CHUNK + 2 * CONV_HALO, CONV_DIM), F32),
        ],
        compiler_params=_cparams(("parallel", "arbitrary")),
        name="ssd_fwd",
    )(xbc, xbc, xbc, dt_raw, xbc_meta, dt_meta, conv_w, conv_b, dtb, alog, dskip)

    rev = lambda b, t: (b * nc + nc - 1 - t, 0)
    return pl.pallas_call(
        _ssd_bwd_body,
        grid=(batch, nc),
        in_specs=[
            pl.BlockSpec((CHUNK, CONV_DIM), rev),
            pl.BlockSpec((CHUNK, DT_LANES), rev),
            pl.BlockSpec((CHUNK, SSM_WIDTH), rev),
            pl.BlockSpec((CHUNK, SSM_WIDTH), rev),
            pl.BlockSpec((1, DT_LANES), const),
            pl.BlockSpec((1, DT_LANES), const),
            pl.BlockSpec((1, SSM_WIDTH), const),
        ],
        out_specs=pl.BlockSpec((CHUNK, SSM_WIDTH), rev),
        out_shape=jax.ShapeDtypeStruct((n, SSM_WIDTH), BF16),
        scratch_shapes=[pltpu.VMEM((N_SSM_HEADS * HEAD_DIM, D_STATE), F32)],
        compiler_params=_cparams(("parallel", "arbitrary")),
        name="ssd_bwd",
    )(xact, dt_raw, ypart, z, dtb, alog, gnorm)


def _outproj_body(oa_ref, os_ref, h_ref, wa_ref, ws_ref, g_ref, o_ref):
    mix = (jnp.dot(oa_ref[...], wa_ref[...], preferred_element_type=F32)
           + jnp.dot(os_ref[...], ws_ref[...], preferred_element_type=F32))
    o_ref[...] = h_ref[...] + _rms(mix, g_ref[...])


def _outproj(o_att, o_ssm, h, w_att, w_ssm, gain, *, tm):
    n = h.shape[0]
    const = lambda i: (0, 0)
    return pl.pallas_call(
        _outproj_body,
        grid=(n // tm,),
        in_specs=[
            pl.BlockSpec((tm, ATT_WIDTH), lambda i: (i, 0)),
            pl.BlockSpec((tm, SSM_WIDTH), lambda i: (i, 0)),
            pl.BlockSpec((tm, D_MODEL), lambda i: (i, 0)),
            pl.BlockSpec((ATT_WIDTH, D_MODEL), const),
            pl.BlockSpec((SSM_WIDTH, D_MODEL), const),
            pl.BlockSpec((1, D_MODEL), const),
        ],
        out_specs=pl.BlockSpec((tm, D_MODEL), lambda i: (i, 0)),
        out_shape=jax.ShapeDtypeStruct((n, D_MODEL), F32),
        compiler_params=_cparams(("parallel",)),
        name="out_proj",
    )(o_att, o_ssm, h, w_att, w_ssm, gain)


def _rope_tables_t(n_tok):
    pos = jnp.arange(n_tok)
    row = (pos // GRID_W).astype(F32)
    col = (pos % GRID_W).astype(F32)
    n_freq = HEAD_DIM // 4
    inv_freq = ROPE_THETA ** (-jnp.arange(n_freq, dtype=F32) / n_freq)
    ang_r = inv_freq[:, None] * row[None, :]
    ang_c = inv_freq[:, None] * col[None, :]
    ang = jnp.concatenate([ang_r, ang_r, ang_c, ang_c], axis=0)
    sign = jnp.concatenate([-jnp.ones((n_freq, 1), F32), jnp.ones((n_freq, 1), F32)] * 2, axis=0)
    return jnp.cos(ang), jnp.sin(ang) * sign


def _tile_of(n, pref):
    return pref if n % pref == 0 else n


def kernel(x_prompt, x_sample, meta_tokens, ff1_norm_pre, ff1_w_gate, ff1_w_up, ff1_w_down, ff1_norm_post,
           mix_norm_pre, w_in, conv_w, conv_b, a_log, dt_bias, d_skip, q_norm, k_norm, ssm_norm, w_out,
           mix_norm_post, ff2_norm_pre, ff2_w_gate, ff2_w_up, ff2_w_down, ff2_norm_post):
    assert ff1_w_gate.shape[0] == 1, "single-layer trunk"
    row2 = lambda v: v.reshape(1, -1).astype(F32)
    ff1 = (row2(ff1_norm_pre[0]), ff1_w_gate[0].astype(BF16), ff1_w_up[0].astype(BF16), ff1_w_down[0].astype(BF16),
           row2(ff1_norm_post[0]))
    ff2 = (row2(ff2_norm_pre[0]), ff2_w_gate[0].astype(BF16), ff2_w_up[0].astype(BF16), ff2_w_down[0].astype(BF16),
           row2(ff2_norm_post[0]))
    w_in0 = w_in[0]
    wqkv_t = w_in0[:, :QKV_WIDTH].T.astype(BF16)
    dt_cols = w_in0[:, QKV_WIDTH + SSM_WIDTH + CONV_DIM:]
    wssm = jnp.concatenate([w_in0[:, QKV_WIDTH:QKV_WIDTH + SSM_WIDTH + CONV_DIM], dt_cols,
                            jnp.zeros((D_MODEL, DT_LANES - dt_cols.shape[1]), F32)], axis=1).astype(BF16)
    g_mix = row2(mix_norm_pre[0])
    gq = q_norm[0].reshape(HEAD_DIM, 1).astype(F32)
    gk = k_norm[0].reshape(HEAD_DIM, 1).astype(F32)
    n_dt = 2 * N_SSM_HEADS
    dtb = jnp.zeros((1, DT_LANES), F32).at[0, :n_dt].set(dt_bias[0].reshape(-1))
    alog = jnp.zeros((1, DT_LANES), F32).at[0, n_dt:2 * n_dt].set(a_log[0].reshape(-1))
    dskip = jnp.repeat(d_skip[0].astype(F32), HEAD_DIM).reshape(1, SSM_WIDTH)
    gnorm = row2(ssm_norm[0])
    cw = conv_w[0].astype(F32)
    cb = row2(conv_b[0])
    w_out_att = w_out[0][:ATT_WIDTH].astype(BF16)
    w_out_ssm = w_out[0][ATT_WIDTH:].astype(BF16)
    g_post = row2(mix_norm_post[0])

    def in_projection(h, cos_t, sin_t, tm):
        return _inproj(h, g_mix, wqkv_t, wssm, cos_t, sin_t, gq, gk, tm=tm)

    x_meta = jnp.concatenate([jnp.zeros((META_LEAD, D_MODEL), F32), meta_tokens.astype(F32)], axis=0)
    h_meta = _ffn(x_meta, *ff1, tm=META_ROWS)
    _, k_chunk, v_t_chunk, _, xbc_meta, dt_meta = in_projection(
        h_meta, jnp.ones((HEAD_DIM, META_ROWS), F32), jnp.zeros((HEAD_DIM, META_ROWS), F32), META_ROWS)
    k_meta = k_chunk[:, META_LEAD:, :]
    v_t_meta = jnp.pad(v_t_chunk[:, :, META_LEAD:], ((0, 0), (0, 0), (0, META_LEAD)))

    def trunk(x):
        batch, seq, _ = x.shape
        n = batch * seq
        tm = _tile_of(n, ROW_TILE)
        h1 = _ffn(x.reshape(n, D_MODEL), *ff1, tm=tm)
        cos_t, sin_t = _rope_tables_t(seq)
        tm_in = _tile_of(seq, ROW_TILE)
        q_t, k, v_t, z, xbc, dt_raw = in_projection(h1, cos_t, sin_t, tm_in)
        o_att = _attention(q_t, k, v_t, k_meta, v_t_meta, batch=batch, seq=seq,
                           tq=_tile_of(seq, ATT_TQ), tk=_tile_of(seq, ATT_TK))
        o_ssm = _ssd(xbc, dt_raw, z, xbc_meta, dt_meta, cw, cb, dtb, alog, dskip, gnorm, batch=batch, seq=seq)
        h2 = _outproj(o_att, o_ssm, h1, w_out_att, w_out_ssm, g_post, tm=tm)
        return _ffn(h2, *ff2, tm=tm).reshape(batch, seq, D_MODEL)

    return trunk(x_prompt), trunk(x_sample)
```

```python
import functools
import math

import jax
import jax.numpy as jnp
from jax import lax
from jax.experimental import pallas as pl
from jax.experimental.pallas import tpu as pltpu

F32 = jnp.float32
BF16 = jnp.bfloat16

D_MODEL = 2048
D_FF = 5632
N_META = 16
GRID_W = 64
EPS = 1e-6
HEAD_DIM = 64
ATT_WIDTH = 1024
N_Q_HEADS = 16
N_KV_HEADS = 4
Q_PER_KV = 4
ROPE_THETA = 10000.0
SSM_WIDTH = 1024
N_SSM_HEADS = 16
N_SSM_GROUPS = 2
HEADS_PER_GROUP = N_SSM_HEADS // N_SSM_GROUPS
D_STATE = 128
D_CONV = 5
CHUNK = 128
CONV_DIM = SSM_WIDTH + 2 * N_SSM_GROUPS * D_STATE
KV_WIDTH = N_KV_HEADS * HEAD_DIM
QKV_WIDTH = ATT_WIDTH + 2 * KV_WIDTH
DT_LANES = 128
SSM_PROJ = SSM_WIDTH + CONV_DIM + DT_LANES

META_ROWS = CHUNK
META_LEAD = CHUNK - N_META
QK_AUG = 128
V_AUG = 80
CONV_HALO = 8

FF_TILE = 512
ROW_TILE = 512
ATT_TQ = 512
ATT_TK = 512
ATT_RING = 4
ATT_EMPTY = -1e30
VMEM_LIMIT = 56 * 1024 * 1024


def _cparams(sem):
    return pltpu.CompilerParams(dimension_semantics=sem, vmem_limit_bytes=VMEM_LIMIT)


def _rms(x, gain):
    ms = jnp.mean(x * x, axis=-1, keepdims=True)
    return x * lax.rsqrt(ms + EPS) * gain


def _ffn_body(x_ref, gpre_ref, wg_ref, wu_ref, wd_ref, gpost_ref, o_ref, u_ref):
    j = pl.program_id(1)

    @pl.when(j == 0)
    def _():
        u_ref[...] = _rms(x_ref[...], gpre_ref[...]).astype(BF16)
        o_ref[...] = jnp.zeros_like(o_ref)

    u = u_ref[...]
    g = jnp.dot(u, wg_ref[...], preferred_element_type=F32)
    up = jnp.dot(u, wu_ref[...], preferred_element_type=F32)
    act = (g * jax.nn.sigmoid(g) * up).astype(BF16)
    o_ref[...] += jnp.dot(act, wd_ref[...], preferred_element_type=F32)

    @pl.when(j == pl.num_programs(1) - 1)
    def _():
        o_ref[...] = x_ref[...] + 0.5 * _rms(o_ref[...], gpost_ref[...])


def _ffn(x, gpre, wg, wu, wd, gpost, *, tm):
    n = x.shape[0]
    return pl.pallas_call(
        _ffn_body,
        grid=(n // tm, D_FF // FF_TILE),
        in_specs=[
            pl.BlockSpec((tm, D_MODEL), lambda i, j: (i, 0)),
            pl.BlockSpec((1, D_MODEL), lambda i, j: (0, 0)),
            pl.BlockSpec((D_MODEL, FF_TILE), lambda i, j: (0, j)),
            pl.BlockSpec((D_MODEL, FF_TILE), lambda i, j: (0, j)),
            pl.BlockSpec((FF_TILE, D_MODEL), lambda i, j: (j, 0)),
            pl.BlockSpec((1, D_MODEL), lambda i, j: (0, 0)),
        ],
        out_specs=pl.BlockSpec((tm, D_MODEL), lambda i, j: (i, 0)),
        out_shape=jax.ShapeDtypeStruct((n, D_MODEL), F32),
        scratch_shapes=[pltpu.VMEM((tm, D_MODEL), BF16)],
        compiler_params=_cparams(("parallel", "arbitrary")),
        name="ffn",
    )(x, gpre, wg, wu, wd, gpost)


def _inproj_body(h_ref, g_ref, wqkv_t_ref, wssm_ref, cos_ref, sin_ref, gq_ref, gk_ref,
                 q_t_ref, k_ref, v_t_ref, z_ref, xbc_ref, dt_ref):
    tm = h_ref.shape[0]
    u = _rms(h_ref[...], g_ref[...]).astype(BF16)
    p_t = lax.dot_general(wqkv_t_ref[...], u, (((1,), (1,)), ((), ())), preferred_element_type=F32)
    ps = jnp.dot(u, wssm_ref[...], preferred_element_type=F32)
    z_ref[...] = ps[:, :SSM_WIDTH]
    xbc_ref[...] = ps[:, SSM_WIDTH:SSM_WIDTH + CONV_DIM]
    dt_ref[...] = ps[:, SSM_WIDTH + CONV_DIM:]

    cos_t = cos_ref[...]
    sin_t = sin_ref[...]

    def norm_rope(x, gain):
        ss = jnp.sum(x * x, axis=0, keepdims=True)
        xg = x * lax.rsqrt(ss * (1.0 / HEAD_DIM) + EPS) * gain
        xs = jnp.concatenate([xg[16:32], xg[0:16], xg[48:64], xg[32:48]], axis=0)
        return xg * cos_t + xs * sin_t

    q_tail = jnp.zeros((QK_AUG - HEAD_DIM, tm), BF16)
    k_tail = jnp.zeros((QK_AUG - HEAD_DIM, tm), F32)
    v_row = lax.broadcasted_iota(jnp.int32, (V_AUG - HEAD_DIM, tm), 0)
    v_tail = jnp.where(v_row == 0, 1.0, 0.0).astype(BF16)
    gq = gq_ref[...]
    gk = gk_ref[...]
    q_scale = HEAD_DIM ** -0.5 * math.log2(math.e)
    for h in range(N_Q_HEADS):
        q = norm_rope(p_t[HEAD_DIM * h:HEAD_DIM * (h + 1)], gq) * q_scale
        q_t_ref[h, 0:HEAD_DIM, :] = q.astype(BF16)
        q_t_ref[h, HEAD_DIM:QK_AUG, :] = q_tail
    for h in range(N_KV_HEADS):
        k = norm_rope(p_t[ATT_WIDTH + HEAD_DIM * h:ATT_WIDTH + HEAD_DIM * (h + 1)], gk)
        k_aug_t = jnp.concatenate([k, k_tail], axis=0)
        k_ref[h] = k_aug_t.T.astype(BF16)
        v0 = ATT_WIDTH + KV_WIDTH + HEAD_DIM * h
        v_t_ref[h, 0:HEAD_DIM, :] = p_t[v0:v0 + HEAD_DIM].astype(BF16)
        v_t_ref[h, HEAD_DIM:V_AUG, :] = v_tail


def _inproj(h, gain, wqkv_t, wssm, cos_t, sin_t, gq, gk, *, tm):
    n = h.shape[0]
    n_tab = cos_t.shape[1] // tm
    const = lambda i: (0, 0)
    return pl.pallas_call(
        _inproj_body,
        grid=(n // tm,),
        in_specs=[
            pl.BlockSpec((tm, D_MODEL), lambda i: (i, 0)),
            pl.BlockSpec((1, D_MODEL), const),
            pl.BlockSpec((QKV_WIDTH, D_MODEL), const, pipeline_mode=pl.Buffered(1)),
            pl.BlockSpec((D_MODEL, SSM_PROJ), const, pipeline_mode=pl.Buffered(1)),
            pl.BlockSpec((HEAD_DIM, tm), lambda i: (0, i % n_tab)),
            pl.BlockSpec((HEAD_DIM, tm), lambda i: (0, i % n_tab)),
            pl.BlockSpec((HEAD_DIM, 1), const),
            pl.BlockSpec((HEAD_DIM, 1), const),
        ],
        out_specs=[
            pl.BlockSpec((N_Q_HEADS, QK_AUG, tm), lambda i: (0, 0, i)),
            pl.BlockSpec((N_KV_HEADS, tm, QK_AUG), lambda i: (0, i, 0)),
            pl.BlockSpec((N_KV_HEADS, None, V_AUG, tm), lambda i: (0, i, 0, 0)),
            pl.BlockSpec((tm, SSM_WIDTH), lambda i: (i, 0)),
            pl.BlockSpec((tm, CONV_DIM), lambda i: (i, 0)),
            pl.BlockSpec((tm, DT_LANES), lambda i: (i, 0)),
        ],
        out_shape=[
            jax.ShapeDtypeStruct((N_Q_HEADS, QK_AUG, n), BF16),
            jax.ShapeDtypeStruct((N_KV_HEADS, n, QK_AUG), BF16),
            jax.ShapeDtypeStruct((N_KV_HEADS, n // tm, V_AUG, tm), BF16),
            jax.ShapeDtypeStruct((n, SSM_WIDTH), F32),
            jax.ShapeDtypeStruct((n, CONV_DIM), F32),
            jax.ShapeDtypeStruct((n, DT_LANES), F32),
        ],
        compiler_params=_cparams(("parallel",)),
        name="in_proj",
    )(h, gain, wqkv_t, wssm, cos_t, sin_t, gq, gk)


def _attn_body(zero_ref, q_t_ref, k_ref, v_t_ref, km_ref, vm_ref, o_ref, m_ref, acc_ref, *ring):
    s_bufs, ml_bufs = ring[:ATT_RING], ring[ATT_RING:]
    tq = q_t_ref.shape[2]
    nk, tk = v_t_ref.shape[1], v_t_ref.shape[3]
    lag = ATT_RING - 1
    rows = pl.ds(pl.multiple_of(zero_ref[0], 8), tk)

    def scores(j, h):
        k_tile = k_ref[h // Q_PER_KV, pl.ds(pl.multiple_of(j * tk, tk), tk), :]
        s = jnp.dot(k_tile, q_t_ref[h], preferred_element_type=F32)
        s_bufs[h % ATT_RING][rows, :] = s
        ml_bufs[h % ATT_RING][...] = jnp.max(s, axis=0, keepdims=True)

    def softmax_pv(j, h):
        m_old = m_ref[h]
        m_new = jnp.maximum(m_old, ml_bufs[h % ATT_RING][...])
        p = jnp.exp2(s_bufs[h % ATT_RING][rows, :] - m_new).astype(BF16)
        pv = jnp.dot(v_t_ref[h // Q_PER_KV, j], p, preferred_element_type=F32)
        acc_ref[h] = acc_ref[h] * jnp.exp2(m_old - m_new) + pv
        m_ref[h] = m_new

    p_pad = jnp.zeros((META_ROWS - N_META, tq), BF16)
    s_all = [jnp.dot(km_ref[h // Q_PER_KV], q_t_ref[h], preferred_element_type=F32) for h in range(N_Q_HEADS)]
    for h, s in enumerate(s_all):
        m = jnp.max(s, axis=0, keepdims=True)
        p = jnp.concatenate([jnp.exp2(s - m).astype(BF16), p_pad], axis=0)
        acc_ref[h] = jnp.dot(vm_ref[h // Q_PER_KV], p, preferred_element_type=F32)
        m_ref[h] = m

    for h in range(N_Q_HEADS - lag, N_Q_HEADS):
        s_bufs[h % ATT_RING][rows, :] = jnp.full((tk, tq), ATT_EMPTY, F32)
        ml_bufs[h % ATT_RING][...] = jnp.full((1, tq), ATT_EMPTY, F32)

    def key_tile(j, carry):
        jm1 = jnp.maximum(j - 1, 0)
        for h in range(N_Q_HEADS):
            d = h - lag
            scores(j, h)
            if d < 0:
                softmax_pv(jm1, d + N_Q_HEADS)
            else:
                softmax_pv(j, d)
        return carry

    lax.fori_loop(0, nk, key_tile, 0)
    for h in range(N_Q_HEADS - lag, N_Q_HEADS):
        softmax_pv(nk - 1, h)

    for g in range(N_KV_HEADS):
        outs = []
        for hh in range(Q_PER_KV):
            acc = acc_ref[g * Q_PER_KV + hh]
            outs.append(acc[0:HEAD_DIM] / acc[HEAD_DIM:HEAD_DIM + 1])
        w = Q_PER_KV * HEAD_DIM
        o_ref[:, g * w:(g + 1) * w] = jnp.concatenate(outs, axis=0).T.astype(BF16)


def _attention(q_t, k, v_t, k_meta, v_t_meta, *, batch, seq, tq):
    n = batch * seq
    nq = seq // tq
    tk = v_t.shape[3]
    nk = seq // tk
    kv_bytes = N_KV_HEADS * seq * (QK_AUG + V_AUG) * 2
    kv_mode = dict(pipeline_mode=pl.Buffered(1)) if 2 * kv_bytes > VMEM_LIMIT // 2 else {}
    grid_spec = pltpu.PrefetchScalarGridSpec(
        num_scalar_prefetch=1,
        grid=(batch, nq),
        in_specs=[
            pl.BlockSpec((N_Q_HEADS, QK_AUG, tq), lambda b, i, z: (0, 0, b * nq + i)),
            pl.BlockSpec((N_KV_HEADS, seq, QK_AUG), lambda b, i, z: (0, b, 0), **kv_mode),
            pl.BlockSpec((N_KV_HEADS, nk, V_AUG, tk), lambda b, i, z: (0, b, 0, 0), **kv_mode),
            pl.BlockSpec((N_KV_HEADS, N_META, QK_AUG), lambda b, i, z: (0, 0, 0)),
            pl.BlockSpec((N_KV_HEADS, V_AUG, META_ROWS), lambda b, i, z: (0, 0, 0)),
        ],
        out_specs=pl.BlockSpec((tq, ATT_WIDTH), lambda b, i, z: (b * nq + i, 0)),
        scratch_shapes=[
            pltpu.VMEM((N_Q_HEADS, 1, tq), F32),
            pltpu.VMEM((N_Q_HEADS, V_AUG, tq), F32),
        ] + [pltpu.VMEM((tk, tq), F32)] * ATT_RING + [pltpu.VMEM((1, tq), F32)] * ATT_RING,
    )
    return pl.pallas_call(
        _attn_body,
        grid_spec=grid_spec,
        out_shape=jax.ShapeDtypeStruct((n, ATT_WIDTH), BF16),
        compiler_params=_cparams(("parallel", "arbitrary")),
        name="attention",
    )(jnp.zeros((1,), jnp.int32), q_t, k, v_t, k_meta, v_t_meta)


def _split3(x):
    hi = x.astype(BF16)
    r1 = x - hi.astype(F32)
    mid = r1.astype(BF16)
    lo = (r1 - mid.astype(F32)).astype(BF16)
    return hi, mid, lo


def _tri_dot(tri, x):
    hi, mid, lo = _split3(x)
    return (jnp.dot(tri, hi, preferred_element_type=F32) + jnp.dot(tri, mid, preferred_element_type=F32)
            + jnp.dot(tri, lo, preferred_element_type=F32))


def _tri_masks():
    r = lax.broadcasted_iota(jnp.int32, (CHUNK, CHUNK), 0)
    c = lax.broadcasted_iota(jnp.int32, (CHUNK, CHUNK), 1)
    return c <= r, c >= r


def _dt_comb(dt_raw, dtb_ref, alog_ref, valid_from=None):
    x = dt_raw + dtb_ref[...]
    sp = jnp.maximum(x, 0.0) + jnp.log(1.0 + jnp.exp(-jnp.abs(x)))
    lane = lax.broadcasted_iota(jnp.int32, sp.shape, 1)
    dtv = jnp.where(lane < 2 * N_SSM_HEADS, sp, 0.0)
    if valid_from is not None:
        row = lax.broadcasted_iota(jnp.int32, sp.shape, 0)
        dtv = jnp.where(row >= valid_from, dtv, 0.0)
    lane1 = lax.broadcasted_iota(jnp.int32, (1, DT_LANES), 1)
    a_mult = jnp.where((lane1 >= 2 * N_SSM_HEADS) & (lane1 < 4 * N_SSM_HEADS), -jnp.exp(alog_ref[...]), 0.0)
    return dtv + pltpu.roll(dtv, 2 * N_SSM_HEADS, axis=1) * a_mult


def _conv_silu(ext_ref, cw_ref, cb_ref):
    acc = cb_ref[...] + ext_ref[pl.ds(CONV_HALO - D_CONV // 2, CHUNK), :] * cw_ref[0:1, :]
    for j in range(1, D_CONV):
        acc = acc + ext_ref[pl.ds(CONV_HALO - D_CONV // 2 + j, CHUNK), :] * cw_ref[j:j + 1, :]
    return acc * jax.nn.sigmoid(acc)


DT_F, DT_B = 0, N_SSM_HEADS
A_F, A_B = 2 * N_SSM_HEADS, 3 * N_SSM_HEADS


def _chunk_states(xs_t, comb_t, cum_t, b_mat, dt_row0, a_row0, end_col):
    dec = jnp.exp(cum_t[a_row0:a_row0 + N_SSM_HEADS, end_col:end_col + 1] - cum_t[a_row0:a_row0 + N_SSM_HEADS])
    out = []
    for g in range(N_SSM_GROUPS):
        rows = []
        for hh in range(HEADS_PER_GROUP):
            h = g * HEADS_PER_GROUP + hh
            xr = xs_t[HEAD_DIM * h:HEAD_DIM * (h + 1)] * (comb_t[dt_row0 + h:dt_row0 + h + 1] * dec[h:h + 1])
            rows.append(xr.astype(BF16))
        xrd = jnp.concatenate(rows, axis=0)
        out.append(jnp.dot(xrd, b_mat[:, D_STATE * g:D_STATE * (g + 1)].astype(BF16), preferred_element_type=F32))
    return out


def _update_state(state_ref, contrib, cum_t, a_row0, end_col):
    cd = jnp.exp(cum_t[a_row0:a_row0 + N_SSM_HEADS, end_col:end_col + 1])
    for g in range(N_SSM_GROUPS):
        for hh in range(HEADS_PER_GROUP):
            h = g * HEADS_PER_GROUP + hh
            sl = slice(HEAD_DIM * h, HEAD_DIM * (h + 1))
            state_ref[sl, :] = state_ref[sl, :] * cd[h:h + 1] + contrib[g][HEAD_DIM * hh:HEAD_DIM * (hh + 1)]


def _ssd_fwd_body(xc_ref, xp_ref, xn_ref, dt_ref, xm_ref, dtm_ref, cw_ref, cb_ref, dtb_ref, alog_ref, dskip_ref,
                  ypart_ref, xact_ref, state_ref, ext_ref):
    t = pl.program_id(1)
    ltri, utri = _tri_masks()
    ltri_b = jnp.where(ltri, 1.0, 0.0).astype(BF16)
    utri_b = jnp.where(utri, 1.0, 0.0).astype(BF16)

    @pl.when(t == 0)
    def _():
        ext_ref[0:CONV_HALO, :] = jnp.zeros((CONV_HALO, CONV_DIM), F32)
        ext_ref[CONV_HALO:CONV_HALO + CHUNK, :] = xm_ref[...]
        ext_ref[CONV_HALO + CHUNK:, :] = xc_ref[0:CONV_HALO, :]
        xact = _conv_silu(ext_ref, cw_ref, cb_ref)
        comb = _dt_comb(dtm_ref[...], dtb_ref, alog_ref, valid_from=META_LEAD)
        pc_t = _tri_dot(ltri_b, comb).T
        contrib = _chunk_states(xact[:, :SSM_WIDTH].T, comb.T, pc_t, xact[:, SSM_WIDTH:SSM_WIDTH + N_SSM_GROUPS * D_STATE],
                                DT_F, A_F, CHUNK - 1)
        for g in range(N_SSM_GROUPS):
            state_ref[HEADS_PER_GROUP * HEAD_DIM * g:HEADS_PER_GROUP * HEAD_DIM * (g + 1), :] = contrib[g]

    ext_ref[0:CONV_HALO, :] = jnp.where(t == 0, xm_ref[CHUNK - CONV_HALO:, :], xp_ref[...])
    ext_ref[CONV_HALO:CONV_HALO + CHUNK, :] = xc_ref[...]
    ext_ref[CONV_HALO + CHUNK:, :] = jnp.where(t == pl.num_programs(1) - 1, 0.0, xn_ref[...])
    xact = _conv_silu(ext_ref, cw_ref, cb_ref)
    xact_ref[...] = xact
    xs = xact[:, :SSM_WIDTH]
    b_mat = xact[:, SSM_WIDTH:SSM_WIDTH + N_SSM_GROUPS * D_STATE]
    c_mat = xact[:, SSM_WIDTH + N_SSM_GROUPS * D_STATE:]

    comb = _dt_comb(dt_ref[...], dtb_ref, alog_ref)
    pc = _tri_dot(ltri_b, comb)
    sc = _tri_dot(utri_b, comb)
    pc_t, sc_t, comb_t = pc.T, sc.T, comb.T
    xs_t = xs.T
    exp_p = jnp.exp(pc_t[A_F:A_F + N_SSM_HEADS])

    y_rows = []
    for g in range(N_SSM_GROUPS):
        bg = b_mat[:, D_STATE * g:D_STATE * (g + 1)].astype(BF16)
        cg_t = c_mat[:, D_STATE * g:D_STATE * (g + 1)].T.astype(BF16)
        cb_t = jnp.dot(bg, cg_t, preferred_element_type=F32)
        cb_f = jnp.where(utri, cb_t, 0.0)
        cb_b = jnp.where(ltri, cb_t, 0.0)
        st = state_ref[HEADS_PER_GROUP * HEAD_DIM * g:HEADS_PER_GROUP * HEAD_DIM * (g + 1), :].astype(BF16)
        yo_t = jnp.dot(st, cg_t, preferred_element_type=F32)
        for hh in range(HEADS_PER_GROUP):
            h = g * HEADS_PER_GROUP + hh
            ef = jnp.exp(jnp.minimum(pc_t[A_F + h:A_F + h + 1] - pc[:, A_F + h:A_F + h + 1], 0.0))
            eb = jnp.exp(jnp.minimum(sc_t[A_B + h:A_B + h + 1] - sc[:, A_B + h:A_B + h + 1], 0.0))
            m_t = jnp.concatenate([(cb_f * ef).astype(BF16), (cb_b * eb).astype(BF16)], axis=0)
            x_h = xs_t[HEAD_DIM * h:HEAD_DIM * (h + 1)]
            xr = jnp.concatenate([(x_h * comb_t[DT_F + h:DT_F + h + 1]).astype(BF16),
                                  (x_h * comb_t[DT_B + h:DT_B + h + 1]).astype(BF16)], axis=1)
            y_h = jnp.dot(xr, m_t, preferred_element_type=F32)
            y_rows.append(y_h + yo_t[HEAD_DIM * hh:HEAD_DIM * (hh + 1)] * exp_p[h:h + 1])
    y_t = jnp.concatenate(y_rows, axis=0)
    ypart_ref[...] = y_t.T + xs * dskip_ref[...]

    contrib = _chunk_states(xs_t, comb_t, pc_t, b_mat, DT_F, A_F, CHUNK - 1)
    _update_state(state_ref, contrib, pc_t, A_F, CHUNK - 1)


def _ssd_bwd_body(xact_ref, dt_ref, ypart_ref, z_ref, dtb_ref, alog_ref, gnorm_ref, o_ref, state_ref):
    t = pl.program_id(1)
    _, utri = _tri_masks()
    utri_b = jnp.where(utri, 1.0, 0.0).astype(BF16)

    @pl.when(t == 0)
    def _():
        state_ref[...] = jnp.zeros_like(state_ref)

    xact = xact_ref[...]
    xs = xact[:, :SSM_WIDTH]
    b_mat = xact[:, SSM_WIDTH:SSM_WIDTH + N_SSM_GROUPS * D_STATE]
    c_mat = xact[:, SSM_WIDTH + N_SSM_GROUPS * D_STATE:]
    comb = _dt_comb(dt_ref[...], dtb_ref, alog_ref)
    sc_t = _tri_dot(utri_b, comb).T
    comb_t = comb.T
    exp_s = jnp.exp(sc_t[A_B:A_B + N_SSM_HEADS])

    y_rows = []
    for g in range(N_SSM_GROUPS):
        cg_t = c_mat[:, D_STATE * g:D_STATE * (g + 1)].T.astype(BF16)
        st = state_ref[HEADS_PER_GROUP * HEAD_DIM * g:HEADS_PER_GROUP * HEAD_DIM * (g + 1), :].astype(BF16)
        yo_t = jnp.dot(st, cg_t, preferred_element_type=F32)
        for hh in range(HEADS_PER_GROUP):
            h = g * HEADS_PER_GROUP + hh
            y_rows.append(yo_t[HEAD_DIM * hh:HEAD_DIM * (hh + 1)] * exp_s[h:h + 1])
    y = ypart_ref[...] + jnp.concatenate(y_rows, axis=0).T
    z = z_ref[...]
    y = y * (z * jax.nn.sigmoid(z))
    gw = SSM_WIDTH // N_SSM_GROUPS
    for g in range(N_SSM_GROUPS):
        o_ref[:, gw * g:gw * (g + 1)] = _rms(y[:, gw * g:gw * (g + 1)], gnorm_ref[:, gw * g:gw * (g + 1)]).astype(BF16)

    contrib = _chunk_states(xs.T, comb_t, sc_t, b_mat, DT_B, A_B, 0)
    _update_state(state_ref, contrib, sc_t, A_B, 0)


def _ssd(xbc, dt_raw, z, xbc_meta, dt_meta, conv_w, conv_b, dtb, alog, dskip, gnorm, *, batch, seq):
    n = batch * seq
    nc = seq // CHUNK
    hb = CHUNK // CONV_HALO
    const = lambda b, t: (0, 0)
    ypart, xact = pl.pallas_call(
        _ssd_fwd_body,
        grid=(batch, nc),
        in_specs=[
            pl.BlockSpec((CHUNK, CONV_DIM), lambda b, t: (b * nc + t, 0)),
            pl.BlockSpec((CONV_HALO, CONV_DIM), lambda b, t: (jnp.maximum((b * nc + t) * hb - 1, 0), 0)),
            pl.BlockSpec((CONV_HALO, CONV_DIM), lambda b, t: (jnp.minimum((b * nc + t + 1) * hb, n // CONV_HALO - 1), 0)),
            pl.BlockSpec((CHUNK, DT_LANES), lambda b, t: (b * nc + t, 0)),
            pl.BlockSpec((META_ROWS, CONV_DIM), const),
            pl.BlockSpec((META_ROWS, DT_LANES), const),
            pl.BlockSpec((D_CONV, CONV_DIM), const),
            pl.BlockSpec((1, CONV_DIM), const),
            pl.BlockSpec((1, DT_LANES), const),
            pl.BlockSpec((1, DT_LANES), const),
            pl.BlockSpec((1, SSM_WIDTH), const),
        ],
        out_specs=[
            pl.BlockSpec((CHUNK, SSM_WIDTH), lambda b, t: (b * nc + t, 0)),
            pl.BlockSpec((CHUNK, CONV_DIM), lambda b, t: (b * nc + t, 0)),
        ],
        out_shape=[
            jax.ShapeDtypeStruct((n, SSM_WIDTH), F32),
            jax.ShapeDtypeStruct((n, CONV_DIM), F32),
        ],
        scratch_shapes=[
            pltpu.VMEM((N_SSM_HEADS * HEAD_DIM, D_STATE), F32),
            pltpu.VMEM((CHUNK + 2 * CONV_HALO, CONV_DIM), F32),
        ],
        compiler_params=_cparams(("parallel", "arbitrary")),
        name="ssd_fwd",
    )(xbc, xbc, xbc, dt_raw, xbc_meta, dt_meta, conv_w, conv_b, dtb, alog, dskip)

    rev = lambda b, t: (b * nc + nc - 1 - t, 0)
    return pl.pallas_call(
        _ssd_bwd_body,
        grid=(batch, nc),
        in_specs=[
            pl.BlockSpec((CHUNK, CONV_DIM), rev),
            pl.BlockSpec((CHUNK, DT_LANES), rev),
            pl.BlockSpec((CHUNK, SSM_WIDTH), rev),
            pl.BlockSpec((CHUNK, SSM_WIDTH), rev),
            pl.BlockSpec((1, DT_LANES), const),
            pl.BlockSpec((1, DT_LANES), const),
            pl.BlockSpec((1, SSM_WIDTH), const),
        ],
        out_specs=pl.BlockSpec((CHUNK, SSM_WIDTH), rev),
        out_shape=jax.ShapeDtypeStruct((n, SSM_WIDTH), BF16),
        scratch_shapes=[pltpu.VMEM((N_SSM_HEADS * HEAD_DIM, D_STATE), F32)],
        compiler_params=_cparams(("parallel", "arbitrary")),
        name="ssd_bwd",
    )(xact, dt_raw, ypart, z, dtb, alog, gnorm)


def _outproj_body(oa_ref, os_ref, h_ref, wa_ref, ws_ref, g_ref, o_ref):
    mix = (jnp.dot(oa_ref[...], wa_ref[...], preferred_element_type=F32)
           + jnp.dot(os_ref[...], ws_ref[...], preferred_element_type=F32))
    o_ref[...] = h_ref[...] + _rms(mix, g_ref[...])


def _outproj(o_att, o_ssm, h, w_att, w_ssm, gain, *, tm):
    n = h.shape[0]
    const = lambda i: (0, 0)
    return pl.pallas_call(
        _outproj_body,
        grid=(n // tm,),
        in_specs=[
            pl.BlockSpec((tm, ATT_WIDTH), lambda i: (i, 0)),
            pl.BlockSpec((tm, SSM_WIDTH), lambda i: (i, 0)),
            pl.BlockSpec((tm, D_MODEL), lambda i: (i, 0)),
            pl.BlockSpec((ATT_WIDTH, D_MODEL), const),
            pl.BlockSpec((SSM_WIDTH, D_MODEL), const),
            pl.BlockSpec((1, D_MODEL), const),
        ],
        out_specs=pl.BlockSpec((tm, D_MODEL), lambda i: (i, 0)),
        out_shape=jax.ShapeDtypeStruct((n, D_MODEL), F32),
        compiler_params=_cparams(("parallel",)),
        name="out_proj",
    )(o_att, o_ssm, h, w_att, w_ssm, gain)


def _rope_tables_t(n_tok):
    pos = jnp.arange(n_tok)
    row = (pos // GRID_W).astype(F32)
    col = (pos % GRID_W).astype(F32)
    n_freq = HEAD_DIM // 4
    inv_freq = ROPE_THETA ** (-jnp.arange(n_freq, dtype=F32) / n_freq)
    ang_r = inv_freq[:, None] * row[None, :]
    ang_c = inv_freq[:, None] * col[None, :]
    ang = jnp.concatenate([ang_r, ang_r, ang_c, ang_c], axis=0)
    sign = jnp.concatenate([-jnp.ones((n_freq, 1), F32), jnp.ones((n_freq, 1), F32)] * 2, axis=0)
    return jnp.cos(ang), jnp.sin(ang) * sign


def _tile_of(n, pref):
    return pref if n % pref == 0 else n


def kernel(x_prompt, x_sample, meta_tokens, ff1_norm_pre, ff1_w_gate, ff1_w_up, ff1_w_down, ff1_norm_post,
           mix_norm_pre, w_in, conv_w, conv_b, a_log, dt_bias, d_skip, q_norm, k_norm, ssm_norm, w_out,
           mix_norm_post, ff2_norm_pre, ff2_w_gate, ff2_w_up, ff2_w_down, ff2_norm_post):
    assert ff1_w_gate.shape[0] == 1, "single-layer trunk"
    row2 = lambda v: v.reshape(1, -1).astype(F32)
    ff1 = (row2(ff1_norm_pre[0]), ff1_w_gate[0].astype(BF16), ff1_w_up[0].astype(BF16), ff1_w_down[0].astype(BF16),
           row2(ff1_norm_post[0]))
    ff2 = (row2(ff2_norm_pre[0]), ff2_w_gate[0].astype(BF16), ff2_w_up[0].astype(BF16), ff2_w_down[0].astype(BF16),
           row2(ff2_norm_post[0]))
    w_in0 = w_in[0]
    wqkv_t = w_in0[:, :QKV_WIDTH].T.astype(BF16)
    dt_cols = w_in0[:, QKV_WIDTH + SSM_WIDTH + CONV_DIM:]
    wssm = jnp.concatenate([w_in0[:, QKV_WIDTH:QKV_WIDTH + SSM_WIDTH + CONV_DIM], dt_cols,
                            jnp.zeros((D_MODEL, DT_LANES - dt_cols.shape[1]), F32)], axis=1).astype(BF16)
    g_mix = row2(mix_norm_pre[0])
    gq = q_norm[0].reshape(HEAD_DIM, 1).astype(F32)
    gk = k_norm[0].reshape(HEAD_DIM, 1).astype(F32)
    n_dt = 2 * N_SSM_HEADS
    dtb = jnp.zeros((1, DT_LANES), F32).at[0, :n_dt].set(dt_bias[0].reshape(-1))
    alog = jnp.zeros((1, DT_LANES), F32).at[0, n_dt:2 * n_dt].set(a_log[0].reshape(-1))
    dskip = jnp.repeat(d_skip[0].astype(F32), HEAD_DIM).reshape(1, SSM_WIDTH)
    gnorm = row2(ssm_norm[0])
    cw = conv_w[0].astype(F32)
    cb = row2(conv_b[0])
    w_out_att = w_out[0][:ATT_WIDTH].astype(BF16)
    w_out_ssm = w_out[0][ATT_WIDTH:].astype(BF16)
    g_post = row2(mix_norm_post[0])

    def in_projection(h, cos_t, sin_t, tm):
        return _inproj(h, g_mix, wqkv_t, wssm, cos_t, sin_t, gq, gk, tm=tm)

    x_meta = jnp.concatenate([jnp.zeros((META_LEAD, D_MODEL), F32), meta_tokens.astype(F32)], axis=0)
    h_meta = _ffn(x_meta, *ff1, tm=META_ROWS)
    _, k_chunk, v_t_chunk, _, xbc_meta, dt_meta = in_projection(
        h_meta, jnp.ones((HEAD_DIM, META_ROWS), F32), jnp.zeros((HEAD_DIM, META_ROWS), F32), META_ROWS)
    k_meta = k_chunk[:, META_LEAD:, :]
    v_t_meta = jnp.pad(v_t_chunk[:, 0, :, META_LEAD:], ((0, 0), (0, 0), (0, META_LEAD)))

    def trunk(x):
        batch, seq, _ = x.shape
        n = batch * seq
        tm = _tile_of(n, ROW_TILE)
        h1 = _ffn(x.reshape(n, D_MODEL), *ff1, tm=tm)
        cos_t, sin_t = _rope_tables_t(seq)
        tm_in = _tile_of(seq, ATT_TK)
        q_t, k, v_t, z, xbc, dt_raw = in_projection(h1, cos_t, sin_t, tm_in)
        o_att = _attention(q_t, k, v_t, k_meta, v_t_meta, batch=batch, seq=seq, tq=_tile_of(seq, ATT_TQ))
        o_ssm = _ssd(xbc, dt_raw, z, xbc_meta, dt_meta, cw, cb, dtb, alog, dskip, gnorm, batch=batch, seq=seq)
        h2 = _outproj(o_att, o_ssm, h1, w_out_att, w_out_ssm, g_post, tm=tm)
        return _ffn(h2, *ff2, tm=tm).reshape(batch, seq, D_MODEL)

    return trunk(x_prompt), trunk(x_sample)
```

```python
import functools
import math

import jax
import jax.numpy as jnp
from jax import lax
from jax.experimental import pallas as pl
from jax.experimental.pallas import tpu as pltpu

F32 = jnp.float32
BF16 = jnp.bfloat16

D_MODEL = 2048
D_FF = 5632
N_META = 16
GRID_W = 64
EPS = 1e-6
HEAD_DIM = 64
ATT_WIDTH = 1024
N_Q_HEADS = 16
N_KV_HEADS = 4
Q_PER_KV = 4
ROPE_THETA = 10000.0
SSM_WIDTH = 1024
N_SSM_HEADS = 16
N_SSM_GROUPS = 2
HEADS_PER_GROUP = N_SSM_HEADS // N_SSM_GROUPS
D_STATE = 128
D_CONV = 5
CHUNK = 128
CONV_DIM = SSM_WIDTH + 2 * N_SSM_GROUPS * D_STATE
KV_WIDTH = N_KV_HEADS * HEAD_DIM
QKV_WIDTH = ATT_WIDTH + 2 * KV_WIDTH
DT_LANES = 128
SSM_PROJ = SSM_WIDTH + CONV_DIM + DT_LANES

META_ROWS = CHUNK
META_LEAD = CHUNK - N_META
QK_AUG = 128
V_AUG = 80
CONV_HALO = 8

FF_TILE = 512
FF_SUB = 2
ROW_TILE = 512
ATT_TQ = 512
ATT_TK = 512
ATT_RING = 4
ATT_EMPTY = -1e30
VMEM_LIMIT = 56 * 1024 * 1024


def _cparams(sem):
    return pltpu.CompilerParams(dimension_semantics=sem, vmem_limit_bytes=VMEM_LIMIT)


def _rms(x, gain):
    ms = jnp.mean(x * x, axis=-1, keepdims=True)
    return x * lax.rsqrt(ms + EPS) * gain


def _ffn_body(x_ref, gpre_ref, wg_ref, wu_ref, wd_ref, gpost_ref, o_ref, u_ref):
    j = pl.program_id(1)

    @pl.when(j == 0)
    def _():
        u_ref[...] = _rms(x_ref[...], gpre_ref[...]).astype(BF16)
        o_ref[...] = jnp.zeros_like(o_ref)

    u = u_ref[...]
    w = FF_TILE // FF_SUB
    gate_up = []
    for c in range(FF_SUB):
        gate_up.append((jnp.dot(u, wg_ref[:, c * w:(c + 1) * w], preferred_element_type=F32),
                        jnp.dot(u, wu_ref[:, c * w:(c + 1) * w], preferred_element_type=F32)))
    acc = o_ref[...]
    for c, (g, up) in enumerate(gate_up):
        act = (g * jax.nn.sigmoid(g) * up).astype(BF16)
        acc = acc + jnp.dot(act, wd_ref[c * w:(c + 1) * w, :], preferred_element_type=F32)
    o_ref[...] = acc

    @pl.when(j == pl.num_programs(1) - 1)
    def _():
        o_ref[...] = x_ref[...] + 0.5 * _rms(o_ref[...], gpost_ref[...])


def _ffn(x, gpre, wg, wu, wd, gpost, *, tm):
    n = x.shape[0]
    return pl.pallas_call(
        _ffn_body,
        grid=(n // tm, D_FF // FF_TILE),
        in_specs=[
            pl.BlockSpec((tm, D_MODEL), lambda i, j: (i, 0)),
            pl.BlockSpec((1, D_MODEL), lambda i, j: (0, 0)),
            pl.BlockSpec((D_MODEL, FF_TILE), lambda i, j: (0, j)),
            pl.BlockSpec((D_MODEL, FF_TILE), lambda i, j: (0, j)),
            pl.BlockSpec((FF_TILE, D_MODEL), lambda i, j: (j, 0)),
            pl.BlockSpec((1, D_MODEL), lambda i, j: (0, 0)),
        ],
        out_specs=pl.BlockSpec((tm, D_MODEL), lambda i, j: (i, 0)),
        out_shape=jax.ShapeDtypeStruct((n, D_MODEL), F32),
        scratch_shapes=[pltpu.VMEM((tm, D_MODEL), BF16)],
        compiler_params=_cparams(("parallel", "arbitrary")),
        name="ffn",
    )(x, gpre, wg, wu, wd, gpost)


def _inproj_body(h_ref, g_ref, wqkv_t_ref, wssm_ref, cos_ref, sin_ref, gq_ref, gk_ref,
                 q_t_ref, k_ref, v_t_ref, z_ref, xbc_ref, dt_ref):
    tm = h_ref.shape[0]
    u = _rms(h_ref[...], g_ref[...]).astype(BF16)
    p_t = lax.dot_general(wqkv_t_ref[...], u, (((1,), (1,)), ((), ())), preferred_element_type=F32)
    ps = jnp.dot(u, wssm_ref[...], preferred_element_type=F32)
    z_ref[...] = ps[:, :SSM_WIDTH]
    xbc_ref[...] = ps[:, SSM_WIDTH:SSM_WIDTH + CONV_DIM]
    dt_ref[...] = ps[:, SSM_WIDTH + CONV_DIM:]

    cos_t = cos_ref[...]
    sin_t = sin_ref[...]

    def norm_rope(x, gain):
        ss = jnp.sum(x * x, axis=0, keepdims=True)
        xg = x * lax.rsqrt(ss * (1.0 / HEAD_DIM) + EPS) * gain
        xs = jnp.concatenate([xg[16:32], xg[0:16], xg[48:64], xg[32:48]], axis=0)
        return xg * cos_t + xs * sin_t

    q_tail = jnp.zeros((QK_AUG - HEAD_DIM, tm), BF16)
    k_tail = jnp.zeros((QK_AUG - HEAD_DIM, tm), F32)
    v_row = lax.broadcasted_iota(jnp.int32, (V_AUG - HEAD_DIM, tm), 0)
    v_tail = jnp.where(v_row == 0, 1.0, 0.0).astype(BF16)
    gq = gq_ref[...]
    gk = gk_ref[...]
    q_scale = HEAD_DIM ** -0.5 * math.log2(math.e)
    for h in range(N_Q_HEADS):
        q = norm_rope(p_t[HEAD_DIM * h:HEAD_DIM * (h + 1)], gq) * q_scale
        q_t_ref[h, 0:HEAD_DIM, :] = q.astype(BF16)
        q_t_ref[h, HEAD_DIM:QK_AUG, :] = q_tail
    for h in range(N_KV_HEADS):
        k = norm_rope(p_t[ATT_WIDTH + HEAD_DIM * h:ATT_WIDTH + HEAD_DIM * (h + 1)], gk)
        k_aug_t = jnp.concatenate([k, k_tail], axis=0)
        k_ref[h] = k_aug_t.T.astype(BF16)
        v0 = ATT_WIDTH + KV_WIDTH + HEAD_DIM * h
        v_t_ref[h, 0:HEAD_DIM, :] = p_t[v0:v0 + HEAD_DIM].astype(BF16)
        v_t_ref[h, HEAD_DIM:V_AUG, :] = v_tail


def _inproj(h, gain, wqkv_t, wssm, cos_t, sin_t, gq, gk, *, tm):
    n = h.shape[0]
    n_tab = cos_t.shape[1] // tm
    const = lambda i: (0, 0)
    return pl.pallas_call(
        _inproj_body,
        grid=(n // tm,),
        in_specs=[
            pl.BlockSpec((tm, D_MODEL), lambda i: (i, 0)),
            pl.BlockSpec((1, D_MODEL), const),
            pl.BlockSpec((QKV_WIDTH, D_MODEL), const, pipeline_mode=pl.Buffered(1)),
            pl.BlockSpec((D_MODEL, SSM_PROJ), const, pipeline_mode=pl.Buffered(1)),
            pl.BlockSpec((HEAD_DIM, tm), lambda i: (0, i % n_tab)),
            pl.BlockSpec((HEAD_DIM, tm), lambda i: (0, i % n_tab)),
            pl.BlockSpec((HEAD_DIM, 1), const),
            pl.BlockSpec((HEAD_DIM, 1), const),
        ],
        out_specs=[
            pl.BlockSpec((N_Q_HEADS, QK_AUG, tm), lambda i: (0, 0, i)),
            pl.BlockSpec((N_KV_HEADS, tm, QK_AUG), lambda i: (0, i, 0)),
            pl.BlockSpec((N_KV_HEADS, None, V_AUG, tm), lambda i: (0, i, 0, 0)),
            pl.BlockSpec((tm, SSM_WIDTH), lambda i: (i, 0)),
            pl.BlockSpec((tm, CONV_DIM), lambda i: (i, 0)),
            pl.BlockSpec((tm, DT_LANES), lambda i: (i, 0)),
        ],
        out_shape=[
            jax.ShapeDtypeStruct((N_Q_HEADS, QK_AUG, n), BF16),
            jax.ShapeDtypeStruct((N_KV_HEADS, n, QK_AUG), BF16),
            jax.ShapeDtypeStruct((N_KV_HEADS, n // tm, V_AUG, tm), BF16),
            jax.ShapeDtypeStruct((n, SSM_WIDTH), F32),
            jax.ShapeDtypeStruct((n, CONV_DIM), F32),
            jax.ShapeDtypeStruct((n, DT_LANES), F32),
        ],
        compiler_params=_cparams(("parallel",)),
        name="in_proj",
    )(h, gain, wqkv_t, wssm, cos_t, sin_t, gq, gk)


def _attn_body(zero_ref, q_t_ref, k_ref, v_t_ref, km_ref, vm_ref, o_ref, m_ref, acc_ref, *ring):
    s_bufs, ml_bufs = ring[:ATT_RING], ring[ATT_RING:]
    tq = q_t_ref.shape[2]
    nk, tk = v_t_ref.shape[1], v_t_ref.shape[3]
    lag = ATT_RING - 1
    rows = pl.ds(pl.multiple_of(zero_ref[0], 8), tk)

    def scores(j, h):
        k_tile = k_ref[h // Q_PER_KV, pl.ds(pl.multiple_of(j * tk, tk), tk), :]
        s = jnp.dot(k_tile, q_t_ref[h], preferred_element_type=F32)
        s_bufs[h % ATT_RING][rows, :] = s
        ml_bufs[h % ATT_RING][...] = jnp.max(s, axis=0, keepdims=True)

    def softmax_pv(j, h):
        m_old = m_ref[h]
        m_new = jnp.maximum(m_old, ml_bufs[h % ATT_RING][...])
        p = jnp.exp2(s_bufs[h % ATT_RING][rows, :] - m_new).astype(BF16)
        pv = jnp.dot(v_t_ref[h // Q_PER_KV, j], p, preferred_element_type=F32)
        acc_ref[h] = acc_ref[h] * jnp.exp2(m_old - m_new) + pv
        m_ref[h] = m_new

    p_pad = jnp.zeros((META_ROWS - N_META, tq), BF16)
    s_all = [jnp.dot(km_ref[h // Q_PER_KV], q_t_ref[h], preferred_element_type=F32) for h in range(N_Q_HEADS)]
    for h, s in enumerate(s_all):
        m = jnp.max(s, axis=0, keepdims=True)
        p = jnp.concatenate([jnp.exp2(s - m).astype(BF16), p_pad], axis=0)
        acc_ref[h] = jnp.dot(vm_ref[h // Q_PER_KV], p, preferred_element_type=F32)
        m_ref[h] = m

    for h in range(N_Q_HEADS - lag, N_Q_HEADS):
        s_bufs[h % ATT_RING][rows, :] = jnp.full((tk, tq), ATT_EMPTY, F32)
        ml_bufs[h % ATT_RING][...] = jnp.full((1, tq), ATT_EMPTY, F32)

    def key_tile(j, carry):
        jm1 = jnp.maximum(j - 1, 0)
        for h in range(N_Q_HEADS):
            d = h - lag
            scores(j, h)
            if d < 0:
                softmax_pv(jm1, d + N_Q_HEADS)
            else:
                softmax_pv(j, d)
        return carry

    lax.fori_loop(0, nk, key_tile, 0)
    for h in range(N_Q_HEADS - lag, N_Q_HEADS):
        softmax_pv(nk - 1, h)

    for g in range(N_KV_HEADS):
        outs = []
        for hh in range(Q_PER_KV):
            acc = acc_ref[g * Q_PER_KV + hh]
            outs.append(acc[0:HEAD_DIM] / acc[HEAD_DIM:HEAD_DIM + 1])
        w = Q_PER_KV * HEAD_DIM
        o_ref[:, g * w:(g + 1) * w] = jnp.concatenate(outs, axis=0).T.astype(BF16)


def _attention(q_t, k, v_t, k_meta, v_t_meta, *, batch, seq, tq):
    n = batch * seq
    nq = seq // tq
    tk = v_t.shape[3]
    nk = seq // tk
    kv_bytes = N_KV_HEADS * seq * (QK_AUG + V_AUG) * 2
    kv_mode = dict(pipeline_mode=pl.Buffered(1)) if 2 * kv_bytes > VMEM_LIMIT // 2 else {}
    grid_spec = pltpu.PrefetchScalarGridSpec(
        num_scalar_prefetch=1,
        grid=(batch, nq),
        in_specs=[
            pl.BlockSpec((N_Q_HEADS, QK_AUG, tq), lambda b, i, z: (0, 0, b * nq + i)),
            pl.BlockSpec((N_KV_HEADS, seq, QK_AUG), lambda b, i, z: (0, b, 0), **kv_mode),
            pl.BlockSpec((N_KV_HEADS, nk, V_AUG, tk), lambda b, i, z: (0, b, 0, 0), **kv_mode),
            pl.BlockSpec((N_KV_HEADS, N_META, QK_AUG), lambda b, i, z: (0, 0, 0)),
            pl.BlockSpec((N_KV_HEADS, V_AUG, META_ROWS), lambda b, i, z: (0, 0, 0)),
        ],
        out_specs=pl.BlockSpec((tq, ATT_WIDTH), lambda b, i, z: (b * nq + i, 0)),
        scratch_shapes=[
            pltpu.VMEM((N_Q_HEADS, 1, tq), F32),
            pltpu.VMEM((N_Q_HEADS, V_AUG, tq), F32),
        ] + [pltpu.VMEM((tk, tq), F32)] * ATT_RING + [pltpu.VMEM((1, tq), F32)] * ATT_RING,
    )
    return pl.pallas_call(
        _attn_body,
        grid_spec=grid_spec,
        out_shape=jax.ShapeDtypeStruct((n, ATT_WIDTH), BF16),
        compiler_params=_cparams(("parallel", "arbitrary")),
        name="attention",
    )(jnp.zeros((1,), jnp.int32), q_t, k, v_t, k_meta, v_t_meta)


def _split3(x):
    hi = x.astype(BF16)
    r1 = x - hi.astype(F32)
    mid = r1.astype(BF16)
    lo = (r1 - mid.astype(F32)).astype(BF16)
    return hi, mid, lo


def _tri_dot(tri, x):
    hi, mid, lo = _split3(x)
    return (jnp.dot(tri, hi, preferred_element_type=F32) + jnp.dot(tri, mid, preferred_element_type=F32)
            + jnp.dot(tri, lo, preferred_element_type=F32))


def _tri_masks():
    r = lax.broadcasted_iota(jnp.int32, (CHUNK, CHUNK), 0)
    c = lax.broadcasted_iota(jnp.int32, (CHUNK, CHUNK), 1)
    return c <= r, c >= r


def _dt_comb(dt_raw, dtb_ref, alog_ref, valid_from=None):
    x = dt_raw + dtb_ref[...]
    sp = jnp.maximum(x, 0.0) + jnp.log(1.0 + jnp.exp(-jnp.abs(x)))
    lane = lax.broadcasted_iota(jnp.int32, sp.shape, 1)
    dtv = jnp.where(lane < 2 * N_SSM_HEADS, sp, 0.0)
    if valid_from is not None:
        row = lax.broadcasted_iota(jnp.int32, sp.shape, 0)
        dtv = jnp.where(row >= valid_from, dtv, 0.0)
    lane1 = lax.broadcasted_iota(jnp.int32, (1, DT_LANES), 1)
    a_mult = jnp.where((lane1 >= 2 * N_SSM_HEADS) & (lane1 < 4 * N_SSM_HEADS), -jnp.exp(alog_ref[...]), 0.0)
    return dtv + pltpu.roll(dtv, 2 * N_SSM_HEADS, axis=1) * a_mult


def _conv_silu(ext_ref, cw_ref, cb_ref):
    acc = cb_ref[...] + ext_ref[pl.ds(CONV_HALO - D_CONV // 2, CHUNK), :] * cw_ref[0:1, :]
    for j in range(1, D_CONV):
        acc = acc + ext_ref[pl.ds(CONV_HALO - D_CONV // 2 + j, CHUNK), :] * cw_ref[j:j + 1, :]
    return acc * jax.nn.sigmoid(acc)


DT_F, DT_B = 0, N_SSM_HEADS
A_F, A_B = 2 * N_SSM_HEADS, 3 * N_SSM_HEADS


def _chunk_states(xs_t, comb_t, cum_t, b_mat, dt_row0, a_row0, end_col):
    dec = jnp.exp(cum_t[a_row0:a_row0 + N_SSM_HEADS, end_col:end_col + 1] - cum_t[a_row0:a_row0 + N_SSM_HEADS])
    out = []
    for g in range(N_SSM_GROUPS):
        rows = []
        for hh in range(HEADS_PER_GROUP):
            h = g * HEADS_PER_GROUP + hh
            xr = xs_t[HEAD_DIM * h:HEAD_DIM * (h + 1)] * (comb_t[dt_row0 + h:dt_row0 + h + 1] * dec[h:h + 1])
            rows.append(xr.astype(BF16))
        xrd = jnp.concatenate(rows, axis=0)
        out.append(jnp.dot(xrd, b_mat[:, D_STATE * g:D_STATE * (g + 1)].astype(BF16), preferred_element_type=F32))
    return out


def _update_state(state_ref, contrib, cum_t, a_row0, end_col):
    cd = jnp.exp(cum_t[a_row0:a_row0 + N_SSM_HEADS, end_col:end_col + 1])
    for g in range(N_SSM_GROUPS):
        for hh in range(HEADS_PER_GROUP):
            h = g * HEADS_PER_GROUP + hh
            sl = slice(HEAD_DIM * h, HEAD_DIM * (h + 1))
            state_ref[sl, :] = state_ref[sl, :] * cd[h:h + 1] + contrib[g][HEAD_DIM * hh:HEAD_DIM * (hh + 1)]


def _ssd_fwd_body(xc_ref, xp_ref, xn_ref, dt_ref, xm_ref, dtm_ref, cw_ref, cb_ref, dtb_ref, alog_ref, dskip_ref,
                  ypart_ref, xact_ref, state_ref, ext_ref):
    t = pl.program_id(1)
    ltri, utri = _tri_masks()
    ltri_b = jnp.where(ltri, 1.0, 0.0).astype(BF16)
    utri_b = jnp.where(utri, 1.0, 0.0).astype(BF16)

    @pl.when(t == 0)
    def _():
        ext_ref[0:CONV_HALO, :] = jnp.zeros((CONV_HALO, CONV_DIM), F32)
        ext_ref[CONV_HALO:CONV_HALO + CHUNK, :] = xm_ref[...]
        ext_ref[CONV_HALO + CHUNK:, :] = xc_ref[0:CONV_HALO, :]
        xact = _conv_silu(ext_ref, cw_ref, cb_ref)
        comb = _dt_comb(dtm_ref[...], dtb_ref, alog_ref, valid_from=META_LEAD)
        pc_t = _tri_dot(ltri_b, comb).T
        contrib = _chunk_states(xact[:, :SSM_WIDTH].T, comb.T, pc_t, xact[:, SSM_WIDTH:SSM_WIDTH + N_SSM_GROUPS * D_STATE],
                                DT_F, A_F, CHUNK - 1)
        for g in range(N_SSM_GROUPS):
            state_ref[HEADS_PER_GROUP * HEAD_DIM * g:HEADS_PER_GROUP * HEAD_DIM * (g + 1), :] = contrib[g]

    ext_ref[0:CONV_HALO, :] = jnp.where(t == 0, xm_ref[CHUNK - CONV_HALO:, :], xp_ref[...])
    ext_ref[CONV_HALO:CONV_HALO + CHUNK, :] = xc_ref[...]
    ext_ref[CONV_HALO + CHUNK:, :] = jnp.where(t == pl.num_programs(1) - 1, 0.0, xn_ref[...])
    xact = _conv_silu(ext_ref, cw_ref, cb_ref)
    xact_ref[...] = xact
    xs = xact[:, :SSM_WIDTH]
    b_mat = xact[:, SSM_WIDTH:SSM_WIDTH + N_SSM_GROUPS * D_STATE]
    c_mat = xact[:, SSM_WIDTH + N_SSM_GROUPS * D_STATE:]

    comb = _dt_comb(dt_ref[...], dtb_ref, alog_ref)
    pc = _tri_dot(ltri_b, comb)
    sc = _tri_dot(utri_b, comb)
    pc_t, sc_t, comb_t = pc.T, sc.T, comb.T
    xs_t = xs.T
    exp_p = jnp.exp(pc_t[A_F:A_F + N_SSM_HEADS])

    y_rows = []
    for g in range(N_SSM_GROUPS):
        bg = b_mat[:, D_STATE * g:D_STATE * (g + 1)].astype(BF16)
        cg_t = c_mat[:, D_STATE * g:D_STATE * (g + 1)].T.astype(BF16)
        cb_t = jnp.dot(bg, cg_t, preferred_element_type=F32)
        cb_f = jnp.where(utri, cb_t, 0.0)
        cb_b = jnp.where(ltri, cb_t, 0.0)
        st = state_ref[HEADS_PER_GROUP * HEAD_DIM * g:HEADS_PER_GROUP * HEAD_DIM * (g + 1), :].astype(BF16)
        yo_t = jnp.dot(st, cg_t, preferred_element_type=F32)
        for hh in range(HEADS_PER_GROUP):
            h = g * HEADS_PER_GROUP + hh
            ef = jnp.exp(jnp.minimum(pc_t[A_F + h:A_F + h + 1] - pc[:, A_F + h:A_F + h + 1], 0.0))
            eb = jnp.exp(jnp.minimum(sc_t[A_B + h:A_B + h + 1] - sc[:, A_B + h:A_B + h + 1], 0.0))
            m_t = jnp.concatenate([(cb_f * ef).astype(BF16), (cb_b * eb).astype(BF16)], axis=0)
            x_h = xs_t[HEAD_DIM * h:HEAD_DIM * (h + 1)]
            xr = jnp.concatenate([(x_h * comb_t[DT_F + h:DT_F + h + 1]).astype(BF16),
                                  (x_h * comb_t[DT_B + h:DT_B + h + 1]).astype(BF16)], axis=1)
            y_h = jnp.dot(xr, m_t, preferred_element_type=F32)
            y_rows.append(y_h + yo_t[HEAD_DIM * hh:HEAD_DIM * (hh + 1)] * exp_p[h:h + 1])
    y_t = jnp.concatenate(y_rows, axis=0)
    ypart_ref[...] = y_t.T + xs * dskip_ref[...]

    contrib = _chunk_states(xs_t, comb_t, pc_t, b_mat, DT_F, A_F, CHUNK - 1)
    _update_state(state_ref, contrib, pc_t, A_F, CHUNK - 1)


def _ssd_bwd_body(xact_ref, dt_ref, ypart_ref, z_ref, dtb_ref, alog_ref, gnorm_ref, o_ref, state_ref):
    t = pl.program_id(1)
    _, utri = _tri_masks()
    utri_b = jnp.where(utri, 1.0, 0.0).astype(BF16)

    @pl.when(t == 0)
    def _():
        state_ref[...] = jnp.zeros_like(state_ref)

    xact = xact_ref[...]
    xs = xact[:, :SSM_WIDTH]
    b_mat = xact[:, SSM_WIDTH:SSM_WIDTH + N_SSM_GROUPS * D_STATE]
    c_mat = xact[:, SSM_WIDTH + N_SSM_GROUPS * D_STATE:]
    comb = _dt_comb(dt_ref[...], dtb_ref, alog_ref)
    sc_t = _tri_dot(utri_b, comb).T
    comb_t = comb.T
    exp_s = jnp.exp(sc_t[A_B:A_B + N_SSM_HEADS])

    y_rows = []
    for g in range(N_SSM_GROUPS):
        cg_t = c_mat[:, D_STATE * g:D_STATE * (g + 1)].T.astype(BF16)
        st = state_ref[HEADS_PER_GROUP * HEAD_DIM * g:HEADS_PER_GROUP * HEAD_DIM * (g + 1), :].astype(BF16)
        yo_t = jnp.dot(st, cg_t, preferred_element_type=F32)
        for hh in range(HEADS_PER_GROUP):
            h = g * HEADS_PER_GROUP + hh
            y_rows.append(yo_t[HEAD_DIM * hh:HEAD_DIM * (hh + 1)] * exp_s[h:h + 1])
    y = ypart_ref[...] + jnp.concatenate(y_rows, axis=0).T
    z = z_ref[...]
    y = y * (z * jax.nn.sigmoid(z))
    gw = SSM_WIDTH // N_SSM_GROUPS
    for g in range(N_SSM_GROUPS):
        o_ref[:, gw * g:gw * (g + 1)] = _rms(y[:, gw * g:gw * (g + 1)], gnorm_ref[:, gw * g:gw * (g + 1)]).astype(BF16)

    contrib = _chunk_states(xs.T, comb_t, sc_t, b_mat, DT_B, A_B, 0)
    _update_state(state_ref, contrib, sc_t, A_B, 0)


def _ssd(xbc, dt_raw, z, xbc_meta, dt_meta, conv_w, conv_b, dtb, alog, dskip, gnorm, *, batch, seq):
    n = batch * seq
    nc = seq // CHUNK
    hb = CHUNK // CONV_HALO
    const = lambda b, t: (0, 0)
    ypart, xact = pl.pallas_call(
        _ssd_fwd_body,
        grid=(batch, nc),
        in_specs=[
            pl.BlockSpec((CHUNK, CONV_DIM), lambda b, t: (b * nc + t, 0)),
            pl.BlockSpec((CONV_HALO, CONV_DIM), lambda b, t: (jnp.maximum((b * nc + t) * hb - 1, 0), 0)),
            pl.BlockSpec((CONV_HALO, CONV_DIM), lambda b, t: (jnp.minimum((b * nc + t + 1) * hb, n // CONV_HALO - 1), 0)),
            pl.BlockSpec((CHUNK, DT_LANES), lambda b, t: (b * nc + t, 0)),
            pl.BlockSpec((META_ROWS, CONV_DIM), const),
            pl.BlockSpec((META_ROWS, DT_LANES), const),
            pl.BlockSpec((D_CONV, CONV_DIM), const),
            pl.BlockSpec((1, CONV_DIM), const),
            pl.BlockSpec((1, DT_LANES), const),
            pl.BlockSpec((1, DT_LANES), const),
            pl.BlockSpec((1, SSM_WIDTH), const),
        ],
        out_specs=[
            pl.BlockSpec((CHUNK, SSM_WIDTH), lambda b, t: (b * nc + t, 0)),
            pl.BlockSpec((CHUNK, CONV_DIM), lambda b, t: (b * nc + t, 0)),
        ],
        out_shape=[
            jax.ShapeDtypeStruct((n, SSM_WIDTH), F32),
            jax.ShapeDtypeStruct((n, CONV_DIM), F32),
        ],
        scratch_shapes=[
            pltpu.VMEM((N_SSM_HEADS * HEAD_DIM, D_STATE), F32),
            pltpu.VMEM((CHUNK + 2 * CONV_HALO, CONV_DIM), F32),
        ],
        compiler_params=_cparams(("parallel", "arbitrary")),
        name="ssd_fwd",
    )(xbc, xbc, xbc, dt_raw, xbc_meta, dt_meta, conv_w, conv_b, dtb, alog, dskip)

    rev = lambda b, t: (b * nc + nc - 1 - t, 0)
    return pl.pallas_call(
        _ssd_bwd_body,
        grid=(batch, nc),
        in_specs=[
            pl.BlockSpec((CHUNK, CONV_DIM), rev),
            pl.BlockSpec((CHUNK, DT_LANES), rev),
            pl.BlockSpec((CHUNK, SSM_WIDTH), rev),
            pl.BlockSpec((CHUNK, SSM_WIDTH), rev),
            pl.BlockSpec((1, DT_LANES), const),
            pl.BlockSpec((1, DT_LANES), const),
            pl.BlockSpec((1, SSM_WIDTH), const),
        ],
        out_specs=pl.BlockSpec((CHUNK, SSM_WIDTH), rev),
        out_shape=jax.ShapeDtypeStruct((n, SSM_WIDTH), BF16),
        scratch_shapes=[pltpu.VMEM((N_SSM_HEADS * HEAD_DIM, D_STATE), F32)],
        compiler_params=_cparams(("parallel", "arbitrary")),
        name="ssd_bwd",
    )(xact, dt_raw, ypart, z, dtb, alog, gnorm)


def _outproj_body(oa_ref, os_ref, h_ref, wa_ref, ws_ref, g_ref, o_ref):
    mix = (jnp.dot(oa_ref[...], wa_ref[...], preferred_element_type=F32)
           + jnp.dot(os_ref[...], ws_ref[...], preferred_element_type=F32))
    o_ref[...] = h_ref[...] + _rms(mix, g_ref[...])


def _outproj(o_att, o_ssm, h, w_att, w_ssm, gain, *, tm):
    n = h.shape[0]
    const = lambda i: (0, 0)
    return pl.pallas_call(
        _outproj_body,
        grid=(n // tm,),
        in_specs=[
            pl.BlockSpec((tm, ATT_WIDTH), lambda i: (i, 0)),
            pl.BlockSpec((tm, SSM_WIDTH), lambda i: (i, 0)),
            pl.BlockSpec((tm, D_MODEL), lambda i: (i, 0)),
            pl.BlockSpec((ATT_WIDTH, D_MODEL), const),
            pl.BlockSpec((SSM_WIDTH, D_MODEL), const),
            pl.BlockSpec((1, D_MODEL), const),
        ],
        out_specs=pl.BlockSpec((tm, D_MODEL), lambda i: (i, 0)),
        out_shape=jax.ShapeDtypeStruct((n, D_MODEL), F32),
        compiler_params=_cparams(("parallel",)),
        name="out_proj",
    )(o_att, o_ssm, h, w_att, w_ssm, gain)


def _rope_tables_t(n_tok):
    pos = jnp.arange(n_tok)
    row = (pos // GRID_W).astype(F32)
    col = (pos % GRID_W).astype(F32)
    n_freq = HEAD_DIM // 4
    inv_freq = ROPE_THETA ** (-jnp.arange(n_freq, dtype=F32) / n_freq)
    ang_r = inv_freq[:, None] * row[None, :]
    ang_c = inv_freq[:, None] * col[None, :]
    ang = jnp.concatenate([ang_r, ang_r, ang_c, ang_c], axis=0)
    sign = jnp.concatenate([-jnp.ones((n_freq, 1), F32), jnp.ones((n_freq, 1), F32)] * 2, axis=0)
    return jnp.cos(ang), jnp.sin(ang) * sign


def _tile_of(n, pref):
    return pref if n % pref == 0 else n


def kernel(x_prompt, x_sample, meta_tokens, ff1_norm_pre, ff1_w_gate, ff1_w_up, ff1_w_down, ff1_norm_post,
           mix_norm_pre, w_in, conv_w, conv_b, a_log, dt_bias, d_skip, q_norm, k_norm, ssm_norm, w_out,
           mix_norm_post, ff2_norm_pre, ff2_w_gate, ff2_w_up, ff2_w_down, ff2_norm_post):
    assert ff1_w_gate.shape[0] == 1, "single-layer trunk"
    row2 = lambda v: v.reshape(1, -1).astype(F32)
    ff1 = (row2(ff1_norm_pre[0]), ff1_w_gate[0].astype(BF16), ff1_w_up[0].astype(BF16), ff1_w_down[0].astype(BF16),
           row2(ff1_norm_post[0]))
    ff2 = (row2(ff2_norm_pre[0]), ff2_w_gate[0].astype(BF16), ff2_w_up[0].astype(BF16), ff2_w_down[0].astype(BF16),
           row2(ff2_norm_post[0]))
    w_in0 = w_in[0]
    wqkv_t = w_in0[:, :QKV_WIDTH].T.astype(BF16)
    dt_cols = w_in0[:, QKV_WIDTH + SSM_WIDTH + CONV_DIM:]
    wssm = jnp.concatenate([w_in0[:, QKV_WIDTH:QKV_WIDTH + SSM_WIDTH + CONV_DIM], dt_cols,
                            jnp.zeros((D_MODEL, DT_LANES - dt_cols.shape[1]), F32)], axis=1).astype(BF16)
    g_mix = row2(mix_norm_pre[0])
    gq = q_norm[0].reshape(HEAD_DIM, 1).astype(F32)
    gk = k_norm[0].reshape(HEAD_DIM, 1).astype(F32)
    n_dt = 2 * N_SSM_HEADS
    dtb = jnp.zeros((1, DT_LANES), F32).at[0, :n_dt].set(dt_bias[0].reshape(-1))
    alog = jnp.zeros((1, DT_LANES), F32).at[0, n_dt:2 * n_dt].set(a_log[0].reshape(-1))
    dskip = jnp.repeat(d_skip[0].astype(F32), HEAD_DIM).reshape(1, SSM_WIDTH)
    gnorm = row2(ssm_norm[0])
    cw = conv_w[0].astype(F32)
    cb = row2(conv_b[0])
    w_out_att = w_out[0][:ATT_WIDTH].astype(BF16)
    w_out_ssm = w_out[0][ATT_WIDTH:].astype(BF16)
    g_post = row2(mix_norm_post[0])

    def in_projection(h, cos_t, sin_t, tm):
        return _inproj(h, g_mix, wqkv_t, wssm, cos_t, sin_t, gq, gk, tm=tm)

    x_meta = jnp.concatenate([jnp.zeros((META_LEAD, D_MODEL), F32), meta_tokens.astype(F32)], axis=0)
    h_meta = _ffn(x_meta, *ff1, tm=META_ROWS)
    _, k_chunk, v_t_chunk, _, xbc_meta, dt_meta = in_projection(
        h_meta, jnp.ones((HEAD_DIM, META_ROWS), F32), jnp.zeros((HEAD_DIM, META_ROWS), F32), META_ROWS)
    k_meta = k_chunk[:, META_LEAD:, :]
    v_t_meta = jnp.pad(v_t_chunk[:, 0, :, META_LEAD:], ((0, 0), (0, 0), (0, META_LEAD)))

    def trunk(x):
        batch, seq, _ = x.shape
        n = batch * seq
        tm = _tile_of(n, ROW_TILE)
        h1 = _ffn(x.reshape(n, D_MODEL), *ff1, tm=tm)
        cos_t, sin_t = _rope_tables_t(seq)
        tm_in = _tile_of(seq, ATT_TK)
        q_t, k, v_t, z, xbc, dt_raw = in_projection(h1, cos_t, sin_t, tm_in)
        o_att = _attention(q_t, k, v_t, k_meta, v_t_meta, batch=batch, seq=seq, tq=_tile_of(seq, ATT_TQ))
        o_ssm = _ssd(xbc, dt_raw, z, xbc_meta, dt_meta, cw, cb, dtb, alog, dskip, gnorm, batch=batch, seq=seq)
        h2 = _outproj(o_att, o_ssm, h1, w_out_att, w_out_ssm, g_post, tm=tm)
        return _ffn(h2, *ff2, tm=tm).reshape(batch, seq, D_MODEL)

    return trunk(x_prompt), trunk(x_sample)
```

```python
import functools
import math

import jax
import jax.numpy as jnp
from jax import lax
from jax.experimental import pallas as pl
from jax.experimental.pallas import tpu as pltpu

F32 = jnp.float32
BF16 = jnp.bfloat16

D_MODEL = 2048
D_FF = 5632
N_META = 16
GRID_W = 64
EPS = 1e-6
HEAD_DIM = 64
ATT_WIDTH = 1024
N_Q_HEADS = 16
N_KV_HEADS = 4
Q_PER_KV = 4
ROPE_THETA = 10000.0
SSM_WIDTH = 1024
N_SSM_HEADS = 16
N_SSM_GROUPS = 2
HEADS_PER_GROUP = N_SSM_HEADS // N_SSM_GROUPS
D_STATE = 128
D_CONV = 5
CHUNK = 128
CONV_DIM = SSM_WIDTH + 2 * N_SSM_GROUPS * D_STATE
KV_WIDTH = N_KV_HEADS * HEAD_DIM
QKV_WIDTH = ATT_WIDTH + 2 * KV_WIDTH
DT_LANES = 128
SSM_PROJ = SSM_WIDTH + CONV_DIM + DT_LANES

META_ROWS = CHUNK
META_LEAD = CHUNK - N_META
QK_AUG = 128
V_AUG = 128
CONV_HALO = 8

FF_TILE = 512
FF_SUB = 2
ROW_TILE = 512
ATT_TQ = 512
ATT_TK = 512
ATT_RING = 4
ATT_EMPTY = -1e30
VMEM_LIMIT = 56 * 1024 * 1024


def _cparams(sem):
    return pltpu.CompilerParams(dimension_semantics=sem, vmem_limit_bytes=VMEM_LIMIT)


def _rms(x, gain):
    ms = jnp.mean(x * x, axis=-1, keepdims=True)
    return x * lax.rsqrt(ms + EPS) * gain


def _ffn_body(x_ref, gpre_ref, wg_ref, wu_ref, wd_ref, gpost_ref, o_ref, u_ref):
    j = pl.program_id(1)

    @pl.when(j == 0)
    def _():
        u_ref[...] = _rms(x_ref[...], gpre_ref[...]).astype(BF16)
        o_ref[...] = jnp.zeros_like(o_ref)

    u = u_ref[...]
    w = FF_TILE // FF_SUB
    gate_up = []
    for c in range(FF_SUB):
        gate_up.append((jnp.dot(u, wg_ref[:, c * w:(c + 1) * w], preferred_element_type=F32),
                        jnp.dot(u, wu_ref[:, c * w:(c + 1) * w], preferred_element_type=F32)))
    acc = o_ref[...]
    for c, (g, up) in enumerate(gate_up):
        act = (g * jax.nn.sigmoid(g) * up).astype(BF16)
        acc = acc + jnp.dot(act, wd_ref[c * w:(c + 1) * w, :], preferred_element_type=F32)
    o_ref[...] = acc

    @pl.when(j == pl.num_programs(1) - 1)
    def _():
        o_ref[...] = x_ref[...] + 0.5 * _rms(o_ref[...], gpost_ref[...])


def _ffn(x, gpre, wg, wu, wd, gpost, *, tm):
    n = x.shape[0]
    return pl.pallas_call(
        _ffn_body,
        grid=(n // tm, D_FF // FF_TILE),
        in_specs=[
            pl.BlockSpec((tm, D_MODEL), lambda i, j: (i, 0)),
            pl.BlockSpec((1, D_MODEL), lambda i, j: (0, 0)),
            pl.BlockSpec((D_MODEL, FF_TILE), lambda i, j: (0, j)),
            pl.BlockSpec((D_MODEL, FF_TILE), lambda i, j: (0, j)),
            pl.BlockSpec((FF_TILE, D_MODEL), lambda i, j: (j, 0)),
            pl.BlockSpec((1, D_MODEL), lambda i, j: (0, 0)),
        ],
        out_specs=pl.BlockSpec((tm, D_MODEL), lambda i, j: (i, 0)),
        out_shape=jax.ShapeDtypeStruct((n, D_MODEL), F32),
        scratch_shapes=[pltpu.VMEM((tm, D_MODEL), BF16)],
        compiler_params=_cparams(("parallel", "arbitrary")),
        name="ffn",
    )(x, gpre, wg, wu, wd, gpost)


def _inproj_body(h_ref, g_ref, wqkv_t_ref, wssm_ref, cos_ref, sin_ref, gq_ref, gk_ref,
                 q_t_ref, k_ref, v_t_ref, z_ref, xbc_ref, dt_ref):
    tm = h_ref.shape[0]
    u = _rms(h_ref[...], g_ref[...]).astype(BF16)
    p_t = lax.dot_general(wqkv_t_ref[...], u, (((1,), (1,)), ((), ())), preferred_element_type=F32)
    ps = jnp.dot(u, wssm_ref[...], preferred_element_type=F32)
    z_ref[...] = ps[:, :SSM_WIDTH]
    xbc_ref[...] = ps[:, SSM_WIDTH:SSM_WIDTH + CONV_DIM]
    dt_ref[...] = ps[:, SSM_WIDTH + CONV_DIM:]

    cos_t = cos_ref[...]
    sin_t = sin_ref[...]

    def norm_rope(x, gain):
        ss = jnp.sum(x * x, axis=0, keepdims=True)
        xg = x * lax.rsqrt(ss * (1.0 / HEAD_DIM) + EPS) * gain
        xs = jnp.concatenate([xg[16:32], xg[0:16], xg[48:64], xg[32:48]], axis=0)
        return xg * cos_t + xs * sin_t

    q_tail = jnp.zeros((QK_AUG - HEAD_DIM, tm), BF16)
    k_tail = jnp.zeros((QK_AUG - HEAD_DIM, tm), F32)
    v_row = lax.broadcasted_iota(jnp.int32, (V_AUG - HEAD_DIM, tm), 0)
    v_tail = jnp.where(v_row == 0, 1.0, 0.0).astype(BF16)
    gq = gq_ref[...]
    gk = gk_ref[...]
    q_scale = HEAD_DIM ** -0.5 * math.log2(math.e)
    for h in range(N_Q_HEADS):
        q = norm_rope(p_t[HEAD_DIM * h:HEAD_DIM * (h + 1)], gq) * q_scale
        q_t_ref[h, 0:HEAD_DIM, :] = q.astype(BF16)
        q_t_ref[h, HEAD_DIM:QK_AUG, :] = q_tail
    for h in range(N_KV_HEADS):
        k = norm_rope(p_t[ATT_WIDTH + HEAD_DIM * h:ATT_WIDTH + HEAD_DIM * (h + 1)], gk)
        k_aug_t = jnp.concatenate([k, k_tail], axis=0)
        k_ref[h] = k_aug_t.T.astype(BF16)
        v0 = ATT_WIDTH + KV_WIDTH + HEAD_DIM * h
        v_t_ref[h, 0:HEAD_DIM, :] = p_t[v0:v0 + HEAD_DIM].astype(BF16)
        v_t_ref[h, HEAD_DIM:V_AUG, :] = v_tail


def _inproj(h, gain, wqkv_t, wssm, cos_t, sin_t, gq, gk, *, tm):
    n = h.shape[0]
    n_tab = cos_t.shape[1] // tm
    const = lambda i: (0, 0)
    return pl.pallas_call(
        _inproj_body,
        grid=(n // tm,),
        in_specs=[
            pl.BlockSpec((tm, D_MODEL), lambda i: (i, 0)),
            pl.BlockSpec((1, D_MODEL), const),
            pl.BlockSpec((QKV_WIDTH, D_MODEL), const, pipeline_mode=pl.Buffered(1)),
            pl.BlockSpec((D_MODEL, SSM_PROJ), const, pipeline_mode=pl.Buffered(1)),
            pl.BlockSpec((HEAD_DIM, tm), lambda i: (0, i % n_tab)),
            pl.BlockSpec((HEAD_DIM, tm), lambda i: (0, i % n_tab)),
            pl.BlockSpec((HEAD_DIM, 1), const),
            pl.BlockSpec((HEAD_DIM, 1), const),
        ],
        out_specs=[
            pl.BlockSpec((N_Q_HEADS, QK_AUG, tm), lambda i: (0, 0, i)),
            pl.BlockSpec((N_KV_HEADS, tm, QK_AUG), lambda i: (0, i, 0)),
            pl.BlockSpec((N_KV_HEADS, None, V_AUG, tm), lambda i: (0, i, 0, 0)),
            pl.BlockSpec((tm, SSM_WIDTH), lambda i: (i, 0)),
            pl.BlockSpec((tm, CONV_DIM), lambda i: (i, 0)),
            pl.BlockSpec((tm, DT_LANES), lambda i: (i, 0)),
        ],
        out_shape=[
            jax.ShapeDtypeStruct((N_Q_HEADS, QK_AUG, n), BF16),
            jax.ShapeDtypeStruct((N_KV_HEADS, n, QK_AUG), BF16),
            jax.ShapeDtypeStruct((N_KV_HEADS, n // tm, V_AUG, tm), BF16),
            jax.ShapeDtypeStruct((n, SSM_WIDTH), F32),
            jax.ShapeDtypeStruct((n, CONV_DIM), F32),
            jax.ShapeDtypeStruct((n, DT_LANES), F32),
        ],
        compiler_params=_cparams(("parallel",)),
        name="in_proj",
    )(h, gain, wqkv_t, wssm, cos_t, sin_t, gq, gk)


def _attn_body(zero_ref, q_t_ref, k_ref, v_t_ref, km_ref, vm_ref, o_ref, m_ref, acc_ref, *ring):
    s_bufs, ml_bufs = ring[:ATT_RING], ring[ATT_RING:]
    tq = q_t_ref.shape[2]
    nk, tk = v_t_ref.shape[1], v_t_ref.shape[3]
    lag = ATT_RING - 1
    rows = pl.ds(pl.multiple_of(zero_ref[0], 8), tk)

    def scores(j, h):
        k_tile = k_ref[h // Q_PER_KV, pl.ds(pl.multiple_of(j * tk, tk), tk), :]
        s = jnp.dot(k_tile, q_t_ref[h], preferred_element_type=F32)
        s_bufs[h % ATT_RING][rows, :] = s
        ml_bufs[h % ATT_RING][...] = jnp.max(s, axis=0, keepdims=True)

    def softmax_pv(j, h):
        m_old = m_ref[h]
        m_new = jnp.maximum(m_old, ml_bufs[h % ATT_RING][...])
        p = jnp.exp2(s_bufs[h % ATT_RING][rows, :] - m_new).astype(BF16)
        pv = jnp.dot(v_t_ref[h // Q_PER_KV, j], p, preferred_element_type=F32)
        acc_ref[h] = acc_ref[h] * jnp.exp2(m_old - m_new) + pv
        m_ref[h] = m_new

    p_pad = jnp.zeros((META_ROWS - N_META, tq), BF16)
    s_all = [jnp.dot(km_ref[h // Q_PER_KV], q_t_ref[h], preferred_element_type=F32) for h in range(N_Q_HEADS)]
    for h, s in enumerate(s_all):
        m = jnp.max(s, axis=0, keepdims=True)
        p = jnp.concatenate([jnp.exp2(s - m).astype(BF16), p_pad], axis=0)
        acc_ref[h] = jnp.dot(vm_ref[h // Q_PER_KV], p, preferred_element_type=F32)
        m_ref[h] = m

    for h in range(N_Q_HEADS - lag, N_Q_HEADS):
        s_bufs[h % ATT_RING][rows, :] = jnp.full((tk, tq), ATT_EMPTY, F32)
        ml_bufs[h % ATT_RING][...] = jnp.full((1, tq), ATT_EMPTY, F32)

    def key_tile(j, carry):
        jm1 = jnp.maximum(j - 1, 0)
        for h in range(N_Q_HEADS):
            d = h - lag
            scores(j, h)
            if d < 0:
                softmax_pv(jm1, d + N_Q_HEADS)
            else:
                softmax_pv(j, d)
        return carry

    lax.fori_loop(0, nk, key_tile, 0)
    for h in range(N_Q_HEADS - lag, N_Q_HEADS):
        softmax_pv(nk - 1, h)

    for g in range(N_KV_HEADS):
        outs = []
        for hh in range(Q_PER_KV):
            acc = acc_ref[g * Q_PER_KV + hh]
            outs.append(acc[0:HEAD_DIM] / acc[HEAD_DIM:HEAD_DIM + 1])
        w = Q_PER_KV * HEAD_DIM
        o_ref[:, g * w:(g + 1) * w] = jnp.concatenate(outs, axis=0).T.astype(BF16)


def _attention(q_t, k, v_t, k_meta, v_t_meta, *, batch, seq, tq):
    n = batch * seq
    nq = seq // tq
    tk = v_t.shape[3]
    nk = seq // tk
    kv_bytes = N_KV_HEADS * seq * (QK_AUG + V_AUG) * 2
    kv_mode = dict(pipeline_mode=pl.Buffered(1)) if 2 * kv_bytes > VMEM_LIMIT // 2 else {}
    grid_spec = pltpu.PrefetchScalarGridSpec(
        num_scalar_prefetch=1,
        grid=(batch, nq),
        in_specs=[
            pl.BlockSpec((N_Q_HEADS, QK_AUG, tq), lambda b, i, z: (0, 0, b * nq + i)),
            pl.BlockSpec((N_KV_HEADS, seq, QK_AUG), lambda b, i, z: (0, b, 0), **kv_mode),
            pl.BlockSpec((N_KV_HEADS, nk, V_AUG, tk), lambda b, i, z: (0, b, 0, 0), **kv_mode),
            pl.BlockSpec((N_KV_HEADS, N_META, QK_AUG), lambda b, i, z: (0, 0, 0)),
            pl.BlockSpec((N_KV_HEADS, V_AUG, META_ROWS), lambda b, i, z: (0, 0, 0)),
        ],
        out_specs=pl.BlockSpec((tq, ATT_WIDTH), lambda b, i, z: (b * nq + i, 0)),
        scratch_shapes=[
            pltpu.VMEM((N_Q_HEADS, 1, tq), F32),
            pltpu.VMEM((N_Q_HEADS, V_AUG, tq), F32),
        ] + [pltpu.VMEM((tk, tq), F32)] * ATT_RING + [pltpu.VMEM((1, tq), F32)] * ATT_RING,
    )
    return pl.pallas_call(
        _attn_body,
        grid_spec=grid_spec,
        out_shape=jax.ShapeDtypeStruct((n, ATT_WIDTH), BF16),
        compiler_params=_cparams(("parallel", "arbitrary")),
        name="attention",
    )(jnp.zeros((1,), jnp.int32), q_t, k, v_t, k_meta, v_t_meta)


def _split3(x):
    hi = x.astype(BF16)
    r1 = x - hi.astype(F32)
    mid = r1.astype(BF16)
    lo = (r1 - mid.astype(F32)).astype(BF16)
    return hi, mid, lo


def _tri_dot(tri, x):
    hi, mid, lo = _split3(x)
    return (jnp.dot(tri, hi, preferred_element_type=F32) + jnp.dot(tri, mid, preferred_element_type=F32)
            + jnp.dot(tri, lo, preferred_element_type=F32))


def _tri_masks():
    r = lax.broadcasted_iota(jnp.int32, (CHUNK, CHUNK), 0)
    c = lax.broadcasted_iota(jnp.int32, (CHUNK, CHUNK), 1)
    return c <= r, c >= r


def _dt_comb(dt_raw, dtb_ref, alog_ref, valid_from=None):
    x = dt_raw + dtb_ref[...]
    sp = jnp.maximum(x, 0.0) + jnp.log(1.0 + jnp.exp(-jnp.abs(x)))
    lane = lax.broadcasted_iota(jnp.int32, sp.shape, 1)
    dtv = jnp.where(lane < 2 * N_SSM_HEADS, sp, 0.0)
    if valid_from is not None:
        row = lax.broadcasted_iota(jnp.int32, sp.shape, 0)
        dtv = jnp.where(row >= valid_from, dtv, 0.0)
    lane1 = lax.broadcasted_iota(jnp.int32, (1, DT_LANES), 1)
    a_mult = jnp.where((lane1 >= 2 * N_SSM_HEADS) & (lane1 < 4 * N_SSM_HEADS), -jnp.exp(alog_ref[...]), 0.0)
    return dtv + pltpu.roll(dtv, 2 * N_SSM_HEADS, axis=1) * a_mult


def _conv_silu(ext_ref, cw_ref, cb_ref):
    acc = cb_ref[...] + ext_ref[pl.ds(CONV_HALO - D_CONV // 2, CHUNK), :] * cw_ref[0:1, :]
    for j in range(1, D_CONV):
        acc = acc + ext_ref[pl.ds(CONV_HALO - D_CONV // 2 + j, CHUNK), :] * cw_ref[j:j + 1, :]
    return acc * jax.nn.sigmoid(acc)


DT_F, DT_B = 0, N_SSM_HEADS
A_F, A_B = 2 * N_SSM_HEADS, 3 * N_SSM_HEADS


def _chunk_states(xs_t, comb_t, cum_t, b_mat, dt_row0, a_row0, end_col):
    dec = jnp.exp(cum_t[a_row0:a_row0 + N_SSM_HEADS, end_col:end_col + 1] - cum_t[a_row0:a_row0 + N_SSM_HEADS])
    out = []
    for g in range(N_SSM_GROUPS):
        rows = []
        for hh in range(HEADS_PER_GROUP):
            h = g * HEADS_PER_GROUP + hh
            xr = xs_t[HEAD_DIM * h:HEAD_DIM * (h + 1)] * (comb_t[dt_row0 + h:dt_row0 + h + 1] * dec[h:h + 1])
            rows.append(xr.astype(BF16))
        xrd = jnp.concatenate(rows, axis=0)
        out.append(jnp.dot(xrd, b_mat[:, D_STATE * g:D_STATE * (g + 1)].astype(BF16), preferred_element_type=F32))
    return out


def _update_state(state_ref, contrib, cum_t, a_row0, end_col):
    cd = jnp.exp(cum_t[a_row0:a_row0 + N_SSM_HEADS, end_col:end_col + 1])
    for g in range(N_SSM_GROUPS):
        for hh in range(HEADS_PER_GROUP):
            h = g * HEADS_PER_GROUP + hh
            sl = slice(HEAD_DIM * h, HEAD_DIM * (h + 1))
            state_ref[sl, :] = state_ref[sl, :] * cd[h:h + 1] + contrib[g][HEAD_DIM * hh:HEAD_DIM * (hh + 1)]


def _ssd_fwd_body(xc_ref, xp_ref, xn_ref, dt_ref, xm_ref, dtm_ref, cw_ref, cb_ref, dtb_ref, alog_ref, dskip_ref,
                  ypart_ref, xact_ref, state_ref, ext_ref):
    t = pl.program_id(1)
    ltri, utri = _tri_masks()
    ltri_b = jnp.where(ltri, 1.0, 0.0).astype(BF16)
    utri_b = jnp.where(utri, 1.0, 0.0).astype(BF16)

    @pl.when(t == 0)
    def _():
        ext_ref[0:CONV_HALO, :] = jnp.zeros((CONV_HALO, CONV_DIM), F32)
        ext_ref[CONV_HALO:CONV_HALO + CHUNK, :] = xm_ref[...]
        ext_ref[CONV_HALO + CHUNK:, :] = xc_ref[0:CONV_HALO, :]
        xact = _conv_silu(ext_ref, cw_ref, cb_ref)
        comb = _dt_comb(dtm_ref[...], dtb_ref, alog_ref, valid_from=META_LEAD)
        pc_t = _tri_dot(ltri_b, comb).T
        contrib = _chunk_states(xact[:, :SSM_WIDTH].T, comb.T, pc_t, xact[:, SSM_WIDTH:SSM_WIDTH + N_SSM_GROUPS * D_STATE],
                                DT_F, A_F, CHUNK - 1)
        for g in range(N_SSM_GROUPS):
            state_ref[HEADS_PER_GROUP * HEAD_DIM * g:HEADS_PER_GROUP * HEAD_DIM * (g + 1), :] = contrib[g]

    ext_ref[0:CONV_HALO, :] = jnp.where(t == 0, xm_ref[CHUNK - CONV_HALO:, :], xp_ref[...])
    ext_ref[CONV_HALO:CONV_HALO + CHUNK, :] = xc_ref[...]
    ext_ref[CONV_HALO + CHUNK:, :] = jnp.where(t == pl.num_programs(1) - 1, 0.0, xn_ref[...])
    xact = _conv_silu(ext_ref, cw_ref, cb_ref)
    xact_ref[...] = xact
    xs = xact[:, :SSM_WIDTH]
    b_mat = xact[:, SSM_WIDTH:SSM_WIDTH + N_SSM_GROUPS * D_STATE]
    c_mat = xact[:, SSM_WIDTH + N_SSM_GROUPS * D_STATE:]

    comb = _dt_comb(dt_ref[...], dtb_ref, alog_ref)
    pc = _tri_dot(ltri_b, comb)
    sc = _tri_dot(utri_b, comb)
    pc_t, sc_t, comb_t = pc.T, sc.T, comb.T
    xs_t = xs.T
    exp_p = jnp.exp(pc_t[A_F:A_F + N_SSM_HEADS])

    y_rows = []
    for g in range(N_SSM_GROUPS):
        bg = b_mat[:, D_STATE * g:D_STATE * (g + 1)].astype(BF16)
        cg_t = c_mat[:, D_STATE * g:D_STATE * (g + 1)].T.astype(BF16)
        cb_t = jnp.dot(bg, cg_t, preferred_element_type=F32)
        cb_f = jnp.where(utri, cb_t, 0.0)
        cb_b = jnp.where(ltri, cb_t, 0.0)
        st = state_ref[HEADS_PER_GROUP * HEAD_DIM * g:HEADS_PER_GROUP * HEAD_DIM * (g + 1), :].astype(BF16)
        yo_t = jnp.dot(st, cg_t, preferred_element_type=F32)
        for hh in range(HEADS_PER_GROUP):
            h = g * HEADS_PER_GROUP + hh
            ef = jnp.exp(jnp.minimum(pc_t[A_F + h:A_F + h + 1] - pc[:, A_F + h:A_F + h + 1], 0.0))
            eb = jnp.exp(jnp.minimum(sc_t[A_B + h:A_B + h + 1] - sc[:, A_B + h:A_B + h + 1], 0.0))
            m_t = jnp.concatenate([(cb_f * ef).astype(BF16), (cb_b * eb).astype(BF16)], axis=0)
            x_h = xs_t[HEAD_DIM * h:HEAD_DIM * (h + 1)]
            xr = jnp.concatenate([(x_h * comb_t[DT_F + h:DT_F + h + 1]).astype(BF16),
                                  (x_h * comb_t[DT_B + h:DT_B + h + 1]).astype(BF16)], axis=1)
            y_h = jnp.dot(xr, m_t, preferred_element_type=F32)
            y_rows.append(y_h + yo_t[HEAD_DIM * hh:HEAD_DIM * (hh + 1)] * exp_p[h:h + 1])
    y_t = jnp.concatenate(y_rows, axis=0)
    ypart_ref[...] = y_t.T + xs * dskip_ref[...]

    contrib = _chunk_states(xs_t, comb_t, pc_t, b_mat, DT_F, A_F, CHUNK - 1)
    _update_state(state_ref, contrib, pc_t, A_F, CHUNK - 1)


def _ssd_bwd_body(xact_ref, dt_ref, ypart_ref, z_ref, dtb_ref, alog_ref, gnorm_ref, o_ref, state_ref):
    t = pl.program_id(1)
    _, utri = _tri_masks()
    utri_b = jnp.where(utri, 1.0, 0.0).astype(BF16)

    @pl.when(t == 0)
    def _():
        state_ref[...] = jnp.zeros_like(state_ref)

    xact = xact_ref[...]
    xs = xact[:, :SSM_WIDTH]
    b_mat = xact[:, SSM_WIDTH:SSM_WIDTH + N_SSM_GROUPS * D_STATE]
    c_mat = xact[:, SSM_WIDTH + N_SSM_GROUPS * D_STATE:]
    comb = _dt_comb(dt_ref[...], dtb_ref, alog_ref)
    sc_t = _tri_dot(utri_b, comb).T
    comb_t = comb.T
    exp_s = jnp.exp(sc_t[A_B:A_B + N_SSM_HEADS])

    y_rows = []
    for g in range(N_SSM_GROUPS):
        cg_t = c_mat[:, D_STATE * g:D_STATE * (g + 1)].T.astype(BF16)
        st = state_ref[HEADS_PER_GROUP * HEAD_DIM * g:HEADS_PER_GROUP * HEAD_DIM * (g + 1), :].astype(BF16)
        yo_t = jnp.dot(st, cg_t, preferred_element_type=F32)
        for hh in range(HEADS_PER_GROUP):
            h = g * HEADS_PER_GROUP + hh
            y_rows.append(yo_t[HEAD_DIM * hh:HEAD_DIM * (hh + 1)] * exp_s[h:h + 1])
    y = ypart_ref[...] + jnp.concatenate(y_rows, axis=0).T
    z = z_ref[...]
    y = y * (z * jax.nn.sigmoid(z))
    gw = SSM_WIDTH // N_SSM_GROUPS
    for g in range(N_SSM_GROUPS):
        o_ref[:, gw * g:gw * (g + 1)] = _rms(y[:, gw * g:gw * (g + 1)], gnorm_ref[:, gw * g:gw * (g + 1)]).astype(BF16)

    contrib = _chunk_states(xs.T, comb_t, sc_t, b_mat, DT_B, A_B, 0)
    _update_state(state_ref, contrib, sc_t, A_B, 0)


def _ssd(xbc, dt_raw, z, xbc_meta, dt_meta, conv_w, conv_b, dtb, alog, dskip, gnorm, *, batch, seq):
    n = batch * seq
    nc = seq // CHUNK
    hb = CHUNK // CONV_HALO
    const = lambda b, t: (0, 0)
    ypart, xact = pl.pallas_call(
        _ssd_fwd_body,
        grid=(batch, nc),
        in_specs=[
            pl.BlockSpec((CHUNK, CONV_DIM), lambda b, t: (b * nc + t, 0)),
            pl.BlockSpec((CONV_HALO, CONV_DIM), lambda b, t: (jnp.maximum((b * nc + t) * hb - 1, 0), 0)),
            pl.BlockSpec((CONV_HALO, CONV_DIM), lambda b, t: (jnp.minimum((b * nc + t + 1) * hb, n // CONV_HALO - 1), 0)),
            pl.BlockSpec((CHUNK, DT_LANES), lambda b, t: (b * nc + t, 0)),
            pl.BlockSpec((META_ROWS, CONV_DIM), const),
            pl.BlockSpec((META_ROWS, DT_LANES), const),
            pl.BlockSpec((D_CONV, CONV_DIM), const),
            pl.BlockSpec((1, CONV_DIM), const),
            pl.BlockSpec((1, DT_LANES), const),
            pl.BlockSpec((1, DT_LANES), const),
            pl.BlockSpec((1, SSM_WIDTH), const),
        ],
        out_specs=[
            pl.BlockSpec((CHUNK, SSM_WIDTH), lambda b, t: (b * nc + t, 0)),
            pl.BlockSpec((CHUNK, CONV_DIM), lambda b, t: (b * nc + t, 0)),
        ],
        out_shape=[
            jax.ShapeDtypeStruct((n, SSM_WIDTH), F32),
            jax.ShapeDtypeStruct((n, CONV_DIM), F32),
        ],
        scratch_shapes=[
            pltpu.VMEM((N_SSM_HEADS * HEAD_DIM, D_STATE), F32),
            pltpu.VMEM((CHUNK + 2 * CONV_HALO, CONV_DIM), F32),
        ],
        compiler_params=_cparams(("parallel", "arbitrary")),
        name="ssd_fwd",
    )(xbc, xbc, xbc, dt_raw, xbc_meta, dt_meta, conv_w, conv_b, dtb, alog, dskip)

    rev = lambda b, t: (b * nc + nc - 1 - t, 0)
    return pl.pallas_call(
        _ssd_bwd_body,
        grid=(batch, nc),
        in_specs=[
            pl.BlockSpec((CHUNK, CONV_DIM), rev),
            pl.BlockSpec((CHUNK, DT_LANES), rev),
            pl.BlockSpec((CHUNK, SSM_WIDTH), rev),
            pl.BlockSpec((CHUNK, SSM_WIDTH), rev),
            pl.BlockSpec((1, DT_LANES), const),
            pl.BlockSpec((1, DT_LANES), const),
            pl.BlockSpec((1, SSM_WIDTH), const),
        ],
        out_specs=pl.BlockSpec((CHUNK, SSM_WIDTH), rev),
        out_shape=jax.ShapeDtypeStruct((n, SSM_WIDTH), BF16),
        scratch_shapes=[pltpu.VMEM((N_SSM_HEADS * HEAD_DIM, D_STATE), F32)],
        compiler_params=_cparams(("parallel", "arbitrary")),
        name="ssd_bwd",
    )(xact, dt_raw, ypart, z, dtb, alog, gnorm)


def _outproj_body(oa_ref, os_ref, h_ref, wa_ref, ws_ref, g_ref, o_ref):
    mix = (jnp.dot(oa_ref[...], wa_ref[...], preferred_element_type=F32)
           + jnp.dot(os_ref[...], ws_ref[...], preferred_element_type=F32))
    o_ref[...] = h_ref[...] + _rms(mix, g_ref[...])


def _outproj(o_att, o_ssm, h, w_att, w_ssm, gain, *, tm):
    n = h.shape[0]
    const = lambda i: (0, 0)
    return pl.pallas_call(
        _outproj_body,
        grid=(n // tm,),
        in_specs=[
            pl.BlockSpec((tm, ATT_WIDTH), lambda i: (i, 0)),
            pl.BlockSpec((tm, SSM_WIDTH), lambda i: (i, 0)),
            pl.BlockSpec((tm, D_MODEL), lambda i: (i, 0)),
            pl.BlockSpec((ATT_WIDTH, D_MODEL), const),
            pl.BlockSpec((SSM_WIDTH, D_MODEL), const),
            pl.BlockSpec((1, D_MODEL), const),
        ],
        out_specs=pl.BlockSpec((tm, D_MODEL), lambda i: (i, 0)),
        out_shape=jax.ShapeDtypeStruct((n, D_MODEL), F32),
        compiler_params=_cparams(("parallel",)),
        name="out_proj",
    )(o_att, o_ssm, h, w_att, w_ssm, gain)


def _rope_tables_t(n_tok):
    pos = jnp.arange(n_tok)
    row = (pos // GRID_W).astype(F32)
    col = (pos % GRID_W).astype(F32)
    n_freq = HEAD_DIM // 4
    inv_freq = ROPE_THETA ** (-jnp.arange(n_freq, dtype=F32) / n_freq)
    ang_r = inv_freq[:, None] * row[None, :]
    ang_c = inv_freq[:, None] * col[None, :]
    ang = jnp.concatenate([ang_r, ang_r, ang_c, ang_c], axis=0)
    sign = jnp.concatenate([-jnp.ones((n_freq, 1), F32), jnp.ones((n_freq, 1), F32)] * 2, axis=0)
    return jnp.cos(ang), jnp.sin(ang) * sign


def _tile_of(n, pref):
    return pref if n % pref == 0 else n


def kernel(x_prompt, x_sample, meta_tokens, ff1_norm_pre, ff1_w_gate, ff1_w_up, ff1_w_down, ff1_norm_post,
           mix_norm_pre, w_in, conv_w, conv_b, a_log, dt_bias, d_skip, q_norm, k_norm, ssm_norm, w_out,
           mix_norm_post, ff2_norm_pre, ff2_w_gate, ff2_w_up, ff2_w_down, ff2_norm_post):
    assert ff1_w_gate.shape[0] == 1, "single-layer trunk"
    row2 = lambda v: v.reshape(1, -1).astype(F32)
    ff1 = (row2(ff1_norm_pre[0]), ff1_w_gate[0].astype(BF16), ff1_w_up[0].astype(BF16), ff1_w_down[0].astype(BF16),
           row2(ff1_norm_post[0]))
    ff2 = (row2(ff2_norm_pre[0]), ff2_w_gate[0].astype(BF16), ff2_w_up[0].astype(BF16), ff2_w_down[0].astype(BF16),
           row2(ff2_norm_post[0]))
    w_in0 = w_in[0]
    wqkv_t = w_in0[:, :QKV_WIDTH].T.astype(BF16)
    dt_cols = w_in0[:, QKV_WIDTH + SSM_WIDTH + CONV_DIM:]
    wssm = jnp.concatenate([w_in0[:, QKV_WIDTH:QKV_WIDTH + SSM_WIDTH + CONV_DIM], dt_cols,
                            jnp.zeros((D_MODEL, DT_LANES - dt_cols.shape[1]), F32)], axis=1).astype(BF16)
    g_mix = row2(mix_norm_pre[0])
    gq = q_norm[0].reshape(HEAD_DIM, 1).astype(F32)
    gk = k_norm[0].reshape(HEAD_DIM, 1).astype(F32)
    n_dt = 2 * N_SSM_HEADS
    dtb = jnp.zeros((1, DT_LANES), F32).at[0, :n_dt].set(dt_bias[0].reshape(-1))
    alog = jnp.zeros((1, DT_LANES), F32).at[0, n_dt:2 * n_dt].set(a_log[0].reshape(-1))
    dskip = jnp.repeat(d_skip[0].astype(F32), HEAD_DIM).reshape(1, SSM_WIDTH)
    gnorm = row2(ssm_norm[0])
    cw = conv_w[0].astype(F32)
    cb = row2(conv_b[0])
    w_out_att = w_out[0][:ATT_WIDTH].astype(BF16)
    w_out_ssm = w_out[0][ATT_WIDTH:].astype(BF16)
    g_post = row2(mix_norm_post[0])

    def in_projection(h, cos_t, sin_t, tm):
        return _inproj(h, g_mix, wqkv_t, wssm, cos_t, sin_t, gq, gk, tm=tm)

    x_meta = jnp.concatenate([jnp.zeros((META_LEAD, D_MODEL), F32), meta_tokens.astype(F32)], axis=0)
    h_meta = _ffn(x_meta, *ff1, tm=META_ROWS)
    _, k_chunk, v_t_chunk, _, xbc_meta, dt_meta = in_projection(
        h_meta, jnp.ones((HEAD_DIM, META_ROWS), F32), jnp.zeros((HEAD_DIM, META_ROWS), F32), META_ROWS)
    k_meta = k_chunk[:, META_LEAD:, :]
    v_t_meta = jnp.pad(v_t_chunk[:, 0, :, META_LEAD:], ((0, 0), (0, 0), (0, META_LEAD)))

    def trunk(x):
        batch, seq, _ = x.shape
        n = batch * seq
        tm = _tile_of(n, ROW_TILE)
        h1 = _ffn(x.reshape(n, D_MODEL), *ff1, tm=tm)
        cos_t, sin_t = _rope_tables_t(seq)
        tm_in = _tile_of(seq, ATT_TK)
        q_t, k, v_t, z, xbc, dt_raw = in_projection(h1, cos_t, sin_t, tm_in)
        o_att = _attention(q_t, k, v_t, k_meta, v_t_meta, batch=batch, seq=seq, tq=_tile_of(seq, ATT_TQ))
        o_ssm = _ssd(xbc, dt_raw, z, xbc_meta, dt_meta, cw, cb, dtb, alog, dskip, gnorm, batch=batch, seq=seq)
        h2 = _outproj(o_att, o_ssm, h1, w_out_att, w_out_ssm, g_post, tm=tm)
        return _ffn(h2, *ff2, tm=tm).reshape(batch, seq, D_MODEL)

    return trunk(x_prompt), trunk(x_sample)
```

```python
import functools
import math

import jax
import jax.numpy as jnp
from jax import lax
from jax.experimental import pallas as pl
from jax.experimental.pallas import tpu as pltpu

F32 = jnp.float32
BF16 = jnp.bfloat16

D_MODEL = 2048
D_FF = 5632
N_META = 16
GRID_W = 64
EPS = 1e-6
HEAD_DIM = 64
ATT_WIDTH = 1024
N_Q_HEADS = 16
N_KV_HEADS = 4
Q_PER_KV = 4
ROPE_THETA = 10000.0
SSM_WIDTH = 1024
N_SSM_HEADS = 16
N_SSM_GROUPS = 2
HEADS_PER_GROUP = N_SSM_HEADS // N_SSM_GROUPS
D_STATE = 128
D_CONV = 5
CHUNK = 128
CONV_DIM = SSM_WIDTH + 2 * N_SSM_GROUPS * D_STATE
KV_WIDTH = N_KV_HEADS * HEAD_DIM
QKV_WIDTH = ATT_WIDTH + 2 * KV_WIDTH
DT_LANES = 128
SSM_PROJ = SSM_WIDTH + CONV_DIM + DT_LANES

META_ROWS = CHUNK
META_LEAD = CHUNK - N_META
QK_AUG = 128
V_AUG = 128
CONV_HALO = 8

FF_TILE = 512
FF_SUB = 2
FFN_ROW_TILE = 1024
FF_ROWS = 512
ROW_TILE = 512
ATT_TQ = 512
ATT_TK = 512
ATT_RING = 4
ATT_EMPTY = -1e30
VMEM_LIMIT = 56 * 1024 * 1024
FFN_VMEM_LIMIT = 60 * 1024 * 1024


def _cparams(sem, vmem_limit=VMEM_LIMIT):
    return pltpu.CompilerParams(dimension_semantics=sem, vmem_limit_bytes=vmem_limit)


def _rms(x, gain):
    ms = jnp.mean(x * x, axis=-1, keepdims=True)
    return x * lax.rsqrt(ms + EPS) * gain


def _ffn_body(x_ref, gpre_ref, wg_ref, wu_ref, wd_ref, gpost_ref, o_ref, u_ref):
    j = pl.program_id(1)

    @pl.when(j == 0)
    def _():
        u_ref[...] = _rms(x_ref[...], gpre_ref[...]).astype(BF16)
        o_ref[...] = jnp.zeros_like(o_ref)

    w = FF_TILE // FF_SUB
    blk = min(x_ref.shape[0], FF_ROWS)
    for r in range(x_ref.shape[0] // blk):
        rows = slice(r * blk, (r + 1) * blk)
        u = u_ref[rows, :]
        gate_up = []
        for c in range(FF_SUB):
            gate_up.append((jnp.dot(u, wg_ref[:, c * w:(c + 1) * w], preferred_element_type=F32),
                            jnp.dot(u, wu_ref[:, c * w:(c + 1) * w], preferred_element_type=F32)))
        acc = o_ref[rows, :]
        for c, (g, up) in enumerate(gate_up):
            act = (g * jax.nn.sigmoid(g) * up).astype(BF16)
            acc = acc + jnp.dot(act, wd_ref[c * w:(c + 1) * w, :], preferred_element_type=F32)
        o_ref[rows, :] = acc

    @pl.when(j == pl.num_programs(1) - 1)
    def _():
        o_ref[...] = x_ref[...] + 0.5 * _rms(o_ref[...], gpost_ref[...])


def _ffn(x, gpre, wg, wu, wd, gpost, *, tm):
    n = x.shape[0]
    return pl.pallas_call(
        _ffn_body,
        grid=(n // tm, D_FF // FF_TILE),
        in_specs=[
            pl.BlockSpec((tm, D_MODEL), lambda i, j: (i, 0)),
            pl.BlockSpec((1, D_MODEL), lambda i, j: (0, 0)),
            pl.BlockSpec((D_MODEL, FF_TILE), lambda i, j: (0, j)),
            pl.BlockSpec((D_MODEL, FF_TILE), lambda i, j: (0, j)),
            pl.BlockSpec((FF_TILE, D_MODEL), lambda i, j: (j, 0)),
            pl.BlockSpec((1, D_MODEL), lambda i, j: (0, 0)),
        ],
        out_specs=pl.BlockSpec((tm, D_MODEL), lambda i, j: (i, 0)),
        out_shape=jax.ShapeDtypeStruct((n, D_MODEL), F32),
        scratch_shapes=[pltpu.VMEM((tm, D_MODEL), BF16)],
        compiler_params=_cparams(("parallel", "arbitrary"), FFN_VMEM_LIMIT),
        name="ffn",
    )(x, gpre, wg, wu, wd, gpost)


def _inproj_body(h_ref, g_ref, wqkv_t_ref, wssm_ref, cos_ref, sin_ref, gq_ref, gk_ref,
                 q_t_ref, k_ref, v_t_ref, z_ref, xbc_ref, dt_ref):
    tm = h_ref.shape[0]
    u = _rms(h_ref[...], g_ref[...]).astype(BF16)
    p_t = lax.dot_general(wqkv_t_ref[...], u, (((1,), (1,)), ((), ())), preferred_element_type=F32)
    ps = jnp.dot(u, wssm_ref[...], preferred_element_type=F32)
    z_ref[...] = ps[:, :SSM_WIDTH]
    xbc_ref[...] = ps[:, SSM_WIDTH:SSM_WIDTH + CONV_DIM]
    dt_ref[...] = ps[:, SSM_WIDTH + CONV_DIM:]

    cos_t = cos_ref[...]
    sin_t = sin_ref[...]

    def norm_rope(x, gain):
        ss = jnp.sum(x * x, axis=0, keepdims=True)
        xg = x * lax.rsqrt(ss * (1.0 / HEAD_DIM) + EPS) * gain
        xs = jnp.concatenate([xg[16:32], xg[0:16], xg[48:64], xg[32:48]], axis=0)
        return xg * cos_t + xs * sin_t

    q_tail = jnp.zeros((QK_AUG - HEAD_DIM, tm), BF16)
    k_tail = jnp.zeros((QK_AUG - HEAD_DIM, tm), F32)
    v_row = lax.broadcasted_iota(jnp.int32, (V_AUG - HEAD_DIM, tm), 0)
    v_tail = jnp.where(v_row == 0, 1.0, 0.0).astype(BF16)
    gq = gq_ref[...]
    gk = gk_ref[...]
    q_scale = HEAD_DIM ** -0.5 * math.log2(math.e)
    for h in range(N_Q_HEADS):
        q = norm_rope(p_t[HEAD_DIM * h:HEAD_DIM * (h + 1)], gq) * q_scale
        q_t_ref[h, 0:HEAD_DIM, :] = q.astype(BF16)
        q_t_ref[h, HEAD_DIM:QK_AUG, :] = q_tail
    for h in range(N_KV_HEADS):
        k = norm_rope(p_t[ATT_WIDTH + HEAD_DIM * h:ATT_WIDTH + HEAD_DIM * (h + 1)], gk)
        k_aug_t = jnp.concatenate([k, k_tail], axis=0)
        k_ref[h] = k_aug_t.T.astype(BF16)
        v0 = ATT_WIDTH + KV_WIDTH + HEAD_DIM * h
        v_t_ref[h, 0:HEAD_DIM, :] = p_t[v0:v0 + HEAD_DIM].astype(BF16)
        v_t_ref[h, HEAD_DIM:V_AUG, :] = v_tail


def _inproj(h, gain, wqkv_t, wssm, cos_t, sin_t, gq, gk, *, tm):
    n = h.shape[0]
    n_tab = cos_t.shape[1] // tm
    const = lambda i: (0, 0)
    return pl.pallas_call(
        _inproj_body,
        grid=(n // tm,),
        in_specs=[
            pl.BlockSpec((tm, D_MODEL), lambda i: (i, 0)),
            pl.BlockSpec((1, D_MODEL), const),
            pl.BlockSpec((QKV_WIDTH, D_MODEL), const, pipeline_mode=pl.Buffered(1)),
            pl.BlockSpec((D_MODEL, SSM_PROJ), const, pipeline_mode=pl.Buffered(1)),
            pl.BlockSpec((HEAD_DIM, tm), lambda i: (0, i % n_tab)),
            pl.BlockSpec((HEAD_DIM, tm), lambda i: (0, i % n_tab)),
            pl.BlockSpec((HEAD_DIM, 1), const),
            pl.BlockSpec((HEAD_DIM, 1), const),
        ],
        out_specs=[
            pl.BlockSpec((N_Q_HEADS, QK_AUG, tm), lambda i: (0, 0, i)),
            pl.BlockSpec((N_KV_HEADS, tm, QK_AUG), lambda i: (0, i, 0)),
            pl.BlockSpec((N_KV_HEADS, None, V_AUG, tm), lambda i: (0, i, 0, 0)),
            pl.BlockSpec((tm, SSM_WIDTH), lambda i: (i, 0)),
            pl.BlockSpec((tm, CONV_DIM), lambda i: (i, 0)),
            pl.BlockSpec((tm, DT_LANES), lambda i: (i, 0)),
        ],
        out_shape=[
            jax.ShapeDtypeStruct((N_Q_HEADS, QK_AUG, n), BF16),
            jax.ShapeDtypeStruct((N_KV_HEADS, n, QK_AUG), BF16),
            jax.ShapeDtypeStruct((N_KV_HEADS, n // tm, V_AUG, tm), BF16),
            jax.ShapeDtypeStruct((n, SSM_WIDTH), F32),
            jax.ShapeDtypeStruct((n, CONV_DIM), F32),
            jax.ShapeDtypeStruct((n, DT_LANES), F32),
        ],
        compiler_params=_cparams(("parallel",)),
        name="in_proj",
    )(h, gain, wqkv_t, wssm, cos_t, sin_t, gq, gk)


def _attn_body(zero_ref, q_t_ref, k_ref, v_t_ref, km_ref, vm_ref, o_ref, m_ref, acc_ref, *ring):
    s_bufs, ml_bufs = ring[:ATT_RING], ring[ATT_RING:]
    tq = q_t_ref.shape[2]
    nk, tk = v_t_ref.shape[1], v_t_ref.shape[3]
    lag = ATT_RING - 1
    rows = pl.ds(pl.multiple_of(zero_ref[0], 8), tk)

    def scores(j, h):
        k_tile = k_ref[h // Q_PER_KV, pl.ds(pl.multiple_of(j * tk, tk), tk), :]
        s = jnp.dot(k_tile, q_t_ref[h], preferred_element_type=F32)
        s_bufs[h % ATT_RING][rows, :] = s
        ml_bufs[h % ATT_RING][...] = jnp.max(s, axis=0, keepdims=True)

    def softmax_pv(j, h):
        m_old = m_ref[h]
        m_new = jnp.maximum(m_old, ml_bufs[h % ATT_RING][...])
        p = jnp.exp2(s_bufs[h % ATT_RING][rows, :] - m_new).astype(BF16)
        pv = jnp.dot(v_t_ref[h // Q_PER_KV, j], p, preferred_element_type=F32)
        acc_ref[h] = acc_ref[h] * jnp.exp2(m_old - m_new) + pv
        m_ref[h] = m_new

    p_pad = jnp.zeros((META_ROWS - N_META, tq), BF16)
    s_all = [jnp.dot(km_ref[h // Q_PER_KV], q_t_ref[h], preferred_element_type=F32) for h in range(N_Q_HEADS)]
    for h, s in enumerate(s_all):
        m = jnp.max(s, axis=0, keepdims=True)
        p = jnp.concatenate([jnp.exp2(s - m).astype(BF16), p_pad], axis=0)
        acc_ref[h] = jnp.dot(vm_ref[h // Q_PER_KV], p, preferred_element_type=F32)
        m_ref[h] = m

    for h in range(N_Q_HEADS - lag, N_Q_HEADS):
        s_bufs[h % ATT_RING][rows, :] = jnp.full((tk, tq), ATT_EMPTY, F32)
        ml_bufs[h % ATT_RING][...] = jnp.full((1, tq), ATT_EMPTY, F32)

    def key_tile(j, carry):
        jm1 = jnp.maximum(j - 1, 0)
        for h in range(N_Q_HEADS):
            d = h - lag
            scores(j, h)
            if d < 0:
                softmax_pv(jm1, d + N_Q_HEADS)
            else:
                softmax_pv(j, d)
        return carry

    lax.fori_loop(0, nk, key_tile, 0)
    for h in range(N_Q_HEADS - lag, N_Q_HEADS):
        softmax_pv(nk - 1, h)

    for g in range(N_KV_HEADS):
        outs = []
        for hh in range(Q_PER_KV):
            acc = acc_ref[g * Q_PER_KV + hh]
            outs.append(acc[0:HEAD_DIM] / acc[HEAD_DIM:HEAD_DIM + 1])
        w = Q_PER_KV * HEAD_DIM
        o_ref[:, g * w:(g + 1) * w] = jnp.concatenate(outs, axis=0).T.astype(BF16)


def _attention(q_t, k, v_t, k_meta, v_t_meta, *, batch, seq, tq):
    n = batch * seq
    nq = seq // tq
    tk = v_t.shape[3]
    nk = seq // tk
    kv_bytes = N_KV_HEADS * seq * (QK_AUG + V_AUG) * 2
    kv_mode = dict(pipeline_mode=pl.Buffered(1)) if 2 * kv_bytes > VMEM_LIMIT // 2 else {}
    grid_spec = pltpu.PrefetchScalarGridSpec(
        num_scalar_prefetch=1,
        grid=(batch, nq),
        in_specs=[
            pl.BlockSpec((N_Q_HEADS, QK_AUG, tq), lambda b, i, z: (0, 0, b * nq + i)),
            pl.BlockSpec((N_KV_HEADS, seq, QK_AUG), lambda b, i, z: (0, b, 0), **kv_mode),
            pl.BlockSpec((N_KV_HEADS, nk, V_AUG, tk), lambda b, i, z: (0, b, 0, 0), **kv_mode),
            pl.BlockSpec((N_KV_HEADS, N_META, QK_AUG), lambda b, i, z: (0, 0, 0)),
            pl.BlockSpec((N_KV_HEADS, V_AUG, META_ROWS), lambda b, i, z: (0, 0, 0)),
        ],
        out_specs=pl.BlockSpec((tq, ATT_WIDTH), lambda b, i, z: (b * nq + i, 0)),
        scratch_shapes=[
            pltpu.VMEM((N_Q_HEADS, 1, tq), F32),
            pltpu.VMEM((N_Q_HEADS, V_AUG, tq), F32),
        ] + [pltpu.VMEM((tk, tq), F32)] * ATT_RING + [pltpu.VMEM((1, tq), F32)] * ATT_RING,
    )
    return pl.pallas_call(
        _attn_body,
        grid_spec=grid_spec,
        out_shape=jax.ShapeDtypeStruct((n, ATT_WIDTH), BF16),
        compiler_params=_cparams(("parallel", "arbitrary")),
        name="attention",
    )(jnp.zeros((1,), jnp.int32), q_t, k, v_t, k_meta, v_t_meta)


def _split3(x):
    hi = x.astype(BF16)
    r1 = x - hi.astype(F32)
    mid = r1.astype(BF16)
    lo = (r1 - mid.astype(F32)).astype(BF16)
    return hi, mid, lo


def _tri_dot(tri, x):
    hi, mid, lo = _split3(x)
    return (jnp.dot(tri, hi, preferred_element_type=F32) + jnp.dot(tri, mid, preferred_element_type=F32)
            + jnp.dot(tri, lo, preferred_element_type=F32))


def _tri_masks():
    r = lax.broadcasted_iota(jnp.int32, (CHUNK, CHUNK), 0)
    c = lax.broadcasted_iota(jnp.int32, (CHUNK, CHUNK), 1)
    return c <= r, c >= r


def _dt_comb(dt_raw, dtb_ref, alog_ref, valid_from=None):
    x = dt_raw + dtb_ref[...]
    sp = jnp.maximum(x, 0.0) + jnp.log(1.0 + jnp.exp(-jnp.abs(x)))
    lane = lax.broadcasted_iota(jnp.int32, sp.shape, 1)
    dtv = jnp.where(lane < 2 * N_SSM_HEADS, sp, 0.0)
    if valid_from is not None:
        row = lax.broadcasted_iota(jnp.int32, sp.shape, 0)
        dtv = jnp.where(row >= valid_from, dtv, 0.0)
    lane1 = lax.broadcasted_iota(jnp.int32, (1, DT_LANES), 1)
    a_mult = jnp.where((lane1 >= 2 * N_SSM_HEADS) & (lane1 < 4 * N_SSM_HEADS), -jnp.exp(alog_ref[...]), 0.0)
    return dtv + pltpu.roll(dtv, 2 * N_SSM_HEADS, axis=1) * a_mult


def _conv_silu(ext_ref, cw_ref, cb_ref):
    acc = cb_ref[...] + ext_ref[pl.ds(CONV_HALO - D_CONV // 2, CHUNK), :] * cw_ref[0:1, :]
    for j in range(1, D_CONV):
        acc = acc + ext_ref[pl.ds(CONV_HALO - D_CONV // 2 + j, CHUNK), :] * cw_ref[j:j + 1, :]
    return acc * jax.nn.sigmoid(acc)


DT_F, DT_B = 0, N_SSM_HEADS
A_F, A_B = 2 * N_SSM_HEADS, 3 * N_SSM_HEADS


def _chunk_states(xs_t, comb_t, cum_t, b_mat, dt_row0, a_row0, end_col):
    dec = jnp.exp(cum_t[a_row0:a_row0 + N_SSM_HEADS, end_col:end_col + 1] - cum_t[a_row0:a_row0 + N_SSM_HEADS])
    out = []
    for g in range(N_SSM_GROUPS):
        rows = []
        for hh in range(HEADS_PER_GROUP):
            h = g * HEADS_PER_GROUP + hh
            xr = xs_t[HEAD_DIM * h:HEAD_DIM * (h + 1)] * (comb_t[dt_row0 + h:dt_row0 + h + 1] * dec[h:h + 1])
            rows.append(xr.astype(BF16))
        xrd = jnp.concatenate(rows, axis=0)
        out.append(jnp.dot(xrd, b_mat[:, D_STATE * g:D_STATE * (g + 1)].astype(BF16), preferred_element_type=F32))
    return out


def _update_state(state_ref, contrib, cum_t, a_row0, end_col):
    cd = jnp.exp(cum_t[a_row0:a_row0 + N_SSM_HEADS, end_col:end_col + 1])
    for g in range(N_SSM_GROUPS):
        for hh in range(HEADS_PER_GROUP):
            h = g * HEADS_PER_GROUP + hh
            sl = slice(HEAD_DIM * h, HEAD_DIM * (h + 1))
            state_ref[sl, :] = state_ref[sl, :] * cd[h:h + 1] + contrib[g][HEAD_DIM * hh:HEAD_DIM * (hh + 1)]


def _ssd_fwd_body(xc_ref, xp_ref, xn_ref, dt_ref, xm_ref, dtm_ref, cw_ref, cb_ref, dtb_ref, alog_ref, dskip_ref,
                  ypart_ref, xact_ref, state_ref, ext_ref):
    t = pl.program_id(1)
    ltri, utri = _tri_masks()
    ltri_b = jnp.where(ltri, 1.0, 0.0).astype(BF16)
    utri_b = jnp.where(utri, 1.0, 0.0).astype(BF16)

    @pl.when(t == 0)
    def _():
        ext_ref[0:CONV_HALO, :] = jnp.zeros((CONV_HALO, CONV_DIM), F32)
        ext_ref[CONV_HALO:CONV_HALO + CHUNK, :] = xm_ref[...]
        ext_ref[CONV_HALO + CHUNK:, :] = xc_ref[0:CONV_HALO, :]
        xact = _conv_silu(ext_ref, cw_ref, cb_ref)
        comb = _dt_comb(dtm_ref[...], dtb_ref, alog_ref, valid_from=META_LEAD)
        pc_t = _tri_dot(ltri_b, comb).T
        contrib = _chunk_states(xact[:, :SSM_WIDTH].T, comb.T, pc_t, xact[:, SSM_WIDTH:SSM_WIDTH + N_SSM_GROUPS * D_STATE],
                                DT_F, A_F, CHUNK - 1)
        for g in range(N_SSM_GROUPS):
            state_ref[HEADS_PER_GROUP * HEAD_DIM * g:HEADS_PER_GROUP * HEAD_DIM * (g + 1), :] = contrib[g]

    ext_ref[0:CONV_HALO, :] = jnp.where(t == 0, xm_ref[CHUNK - CONV_HALO:, :], xp_ref[...])
    ext_ref[CONV_HALO:CONV_HALO + CHUNK, :] = xc_ref[...]
    ext_ref[CONV_HALO + CHUNK:, :] = jnp.where(t == pl.num_programs(1) - 1, 0.0, xn_ref[...])
    xact = _conv_silu(ext_ref, cw_ref, cb_ref)
    xact_ref[...] = xact
    xs = xact[:, :SSM_WIDTH]
    b_mat = xact[:, SSM_WIDTH:SSM_WIDTH + N_SSM_GROUPS * D_STATE]
    c_mat = xact[:, SSM_WIDTH + N_SSM_GROUPS * D_STATE:]

    comb = _dt_comb(dt_ref[...], dtb_ref, alog_ref)
    pc = _tri_dot(ltri_b, comb)
    sc = _tri_dot(utri_b, comb)
    pc_t, sc_t, comb_t = pc.T, sc.T, comb.T
    xs_t = xs.T
    exp_p = jnp.exp(pc_t[A_F:A_F + N_SSM_HEADS])

    y_rows = []
    for g in range(N_SSM_GROUPS):
        bg = b_mat[:, D_STATE * g:D_STATE * (g + 1)].astype(BF16)
        cg_t = c_mat[:, D_STATE * g:D_STATE * (g + 1)].T.astype(BF16)
        cb_t = jnp.dot(bg, cg_t, preferred_element_type=F32)
        cb_f = jnp.where(utri, cb_t, 0.0)
        cb_b = jnp.where(ltri, cb_t, 0.0)
        st = state_ref[HEADS_PER_GROUP * HEAD_DIM * g:HEADS_PER_GROUP * HEAD_DIM * (g + 1), :].astype(BF16)
        yo_t = jnp.dot(st, cg_t, preferred_element_type=F32)
        for hh in range(HEADS_PER_GROUP):
            h = g * HEADS_PER_GROUP + hh
            ef = jnp.exp(jnp.minimum(pc_t[A_F + h:A_F + h + 1] - pc[:, A_F + h:A_F + h + 1], 0.0))
            eb = jnp.exp(jnp.minimum(sc_t[A_B + h:A_B + h + 1] - sc[:, A_B + h:A_B + h + 1], 0.0))
            m_t = jnp.concatenate([(cb_f * ef).astype(BF16), (cb_b * eb).astype(BF16)], axis=0)
            x_h = xs_t[HEAD_DIM * h:HEAD_DIM * (h + 1)]
            xr = jnp.concatenate([(x_h * comb_t[DT_F + h:DT_F + h + 1]).astype(BF16),
                                  (x_h * comb_t[DT_B + h:DT_B + h + 1]).astype(BF16)], axis=1)
            y_h = jnp.dot(xr, m_t, preferred_element_type=F32)
            y_rows.append(y_h + yo_t[HEAD_DIM * hh:HEAD_DIM * (hh + 1)] * exp_p[h:h + 1])
    y_t = jnp.concatenate(y_rows, axis=0)
    ypart_ref[...] = y_t.T + xs * dskip_ref[...]

    contrib = _chunk_states(xs_t, comb_t, pc_t, b_mat, DT_F, A_F, CHUNK - 1)
    _update_state(state_ref, contrib, pc_t, A_F, CHUNK - 1)


def _ssd_bwd_body(xact_ref, dt_ref, ypart_ref, z_ref, dtb_ref, alog_ref, gnorm_ref, o_ref, state_ref):
    t = pl.program_id(1)
    _, utri = _tri_masks()
    utri_b = jnp.where(utri, 1.0, 0.0).astype(BF16)

    @pl.when(t == 0)
    def _():
        state_ref[...] = jnp.zeros_like(state_ref)

    xact = xact_ref[...]
    xs = xact[:, :SSM_WIDTH]
    b_mat = xact[:, SSM_WIDTH:SSM_WIDTH + N_SSM_GROUPS * D_STATE]
    c_mat = xact[:, SSM_WIDTH + N_SSM_GROUPS * D_STATE:]
    comb = _dt_comb(dt_ref[...], dtb_ref, alog_ref)
    sc_t = _tri_dot(utri_b, comb).T
    comb_t = comb.T
    exp_s = jnp.exp(sc_t[A_B:A_B + N_SSM_HEADS])

    y_rows = []
    for g in range(N_SSM_GROUPS):
        cg_t = c_mat[:, D_STATE * g:D_STATE * (g + 1)].T.astype(BF16)
        st = state_ref[HEADS_PER_GROUP * HEAD_DIM * g:HEADS_PER_GROUP * HEAD_DIM * (g + 1), :].astype(BF16)
        yo_t = jnp.dot(st, cg_t, preferred_element_type=F32)
        for hh in range(HEADS_PER_GROUP):
            h = g * HEADS_PER_GROUP + hh
            y_rows.append(yo_t[HEAD_DIM * hh:HEAD_DIM * (hh + 1)] * exp_s[h:h + 1])
    y = ypart_ref[...] + jnp.concatenate(y_rows, axis=0).T
    z = z_ref[...]
    y = y * (z * jax.nn.sigmoid(z))
    gw = SSM_WIDTH // N_SSM_GROUPS
    for g in range(N_SSM_GROUPS):
        o_ref[:, gw * g:gw * (g + 1)] = _rms(y[:, gw * g:gw * (g + 1)], gnorm_ref[:, gw * g:gw * (g + 1)]).astype(BF16)

    contrib = _chunk_states(xs.T, comb_t, sc_t, b_mat, DT_B, A_B, 0)
    _update_state(state_ref, contrib, sc_t, A_B, 0)


def _ssd(xbc, dt_raw, z, xbc_meta, dt_meta, conv_w, conv_b, dtb, alog, dskip, gnorm, *, batch, seq):
    n = batch * seq
    nc = seq // CHUNK
    hb = CHUNK // CONV_HALO
    const = lambda b, t: (0, 0)
    ypart, xact = pl.pallas_call(
        _ssd_fwd_body,
        grid=(batch, nc),
        in_specs=[
            pl.BlockSpec((CHUNK, CONV_DIM), lambda b, t: (b * nc + t, 0)),
            pl.BlockSpec((CONV_HALO, CONV_DIM), lambda b, t: (jnp.maximum((b * nc + t) * hb - 1, 0), 0)),
            pl.BlockSpec((CONV_HALO, CONV_DIM), lambda b, t: (jnp.minimum((b * nc + t + 1) * hb, n // CONV_HALO - 1), 0)),
            pl.BlockSpec((CHUNK, DT_LANES), lambda b, t: (b * nc + t, 0)),
            pl.BlockSpec((META_ROWS, CONV_DIM), const),
            pl.BlockSpec((META_ROWS, DT_LANES), const),
            pl.BlockSpec((D_CONV, CONV_DIM), const),
            pl.BlockSpec((1, CONV_DIM), const),
            pl.BlockSpec((1, DT_LANES), const),
            pl.BlockSpec((1, DT_LANES), const),
            pl.BlockSpec((1, SSM_WIDTH), const),
        ],
        out_specs=[
            pl.BlockSpec((CHUNK, SSM_WIDTH), lambda b, t: (b * nc + t, 0)),
            pl.BlockSpec((CHUNK, CONV_DIM), lambda b, t: (b * nc + t, 0)),
        ],
        out_shape=[
            jax.ShapeDtypeStruct((n, SSM_WIDTH), F32),
            jax.ShapeDtypeStruct((n, CONV_DIM), F32),
        ],
        scratch_shapes=[
            pltpu.VMEM((N_SSM_HEADS * HEAD_DIM, D_STATE), F32),
            pltpu.VMEM((CHUNK + 2 * CONV_HALO, CONV_DIM), F32),
        ],
        compiler_params=_cparams(("parallel", "arbitrary")),
        name="ssd_fwd",
    )(xbc, xbc, xbc, dt_raw, xbc_meta, dt_meta, conv_w, conv_b, dtb, alog, dskip)

    rev = lambda b, t: (b * nc + nc - 1 - t, 0)
    return pl.pallas_call(
        _ssd_bwd_body,
        grid=(batch, nc),
        in_specs=[
            pl.BlockSpec((CHUNK, CONV_DIM), rev),
            pl.BlockSpec((CHUNK, DT_LANES), rev),
            pl.BlockSpec((CHUNK, SSM_WIDTH), rev),
            pl.BlockSpec((CHUNK, SSM_WIDTH), rev),
            pl.BlockSpec((1, DT_LANES), const),
            pl.BlockSpec((1, DT_LANES), const),
            pl.BlockSpec((1, SSM_WIDTH), const),
        ],
        out_specs=pl.BlockSpec((CHUNK, SSM_WIDTH), rev),
        out_shape=jax.ShapeDtypeStruct((n, SSM_WIDTH), BF16),
        scratch_shapes=[pltpu.VMEM((N_SSM_HEADS * HEAD_DIM, D_STATE), F32)],
        compiler_params=_cparams(("parallel", "arbitrary")),
        name="ssd_bwd",
    )(xact, dt_raw, ypart, z, dtb, alog, gnorm)


def _outproj_body(oa_ref, os_ref, h_ref, wa_ref, ws_ref, g_ref, o_ref):
    mix = (jnp.dot(oa_ref[...], wa_ref[...], preferred_element_type=F32)
           + jnp.dot(os_ref[...], ws_ref[...], preferred_element_type=F32))
    o_ref[...] = h_ref[...] + _rms(mix, g_ref[...])


def _outproj(o_att, o_ssm, h, w_att, w_ssm, gain, *, tm):
    n = h.shape[0]
    const = lambda i: (0, 0)
    return pl.pallas_call(
        _outproj_body,
        grid=(n // tm,),
        in_specs=[
            pl.BlockSpec((tm, ATT_WIDTH), lambda i: (i, 0)),
            pl.BlockSpec((tm, SSM_WIDTH), lambda i: (i, 0)),
            pl.BlockSpec((tm, D_MODEL), lambda i: (i, 0)),
            pl.BlockSpec((ATT_WIDTH, D_MODEL), const),
            pl.BlockSpec((SSM_WIDTH, D_MODEL), const),
            pl.BlockSpec((1, D_MODEL), const),
        ],
        out_specs=pl.BlockSpec((tm, D_MODEL), lambda i: (i, 0)),
        out_shape=jax.ShapeDtypeStruct((n, D_MODEL), F32),
        compiler_params=_cparams(("parallel",)),
        name="out_proj",
    )(o_att, o_ssm, h, w_att, w_ssm, gain)


def _rope_tables_t(n_tok):
    pos = jnp.arange(n_tok)
    row = (pos // GRID_W).astype(F32)
    col = (pos % GRID_W).astype(F32)
    n_freq = HEAD_DIM // 4
    inv_freq = ROPE_THETA ** (-jnp.arange(n_freq, dtype=F32) / n_freq)
    ang_r = inv_freq[:, None] * row[None, :]
    ang_c = inv_freq[:, None] * col[None, :]
    ang = jnp.concatenate([ang_r, ang_r, ang_c, ang_c], axis=0)
    sign = jnp.concatenate([-jnp.ones((n_freq, 1), F32), jnp.ones((n_freq, 1), F32)] * 2, axis=0)
    return jnp.cos(ang), jnp.sin(ang) * sign


def _tile_of(n, pref):
    return pref if n % pref == 0 else n


def kernel(x_prompt, x_sample, meta_tokens, ff1_norm_pre, ff1_w_gate, ff1_w_up, ff1_w_down, ff1_norm_post,
           mix_norm_pre, w_in, conv_w, conv_b, a_log, dt_bias, d_skip, q_norm, k_norm, ssm_norm, w_out,
           mix_norm_post, ff2_norm_pre, ff2_w_gate, ff2_w_up, ff2_w_down, ff2_norm_post):
    assert ff1_w_gate.shape[0] == 1, "single-layer trunk"
    row2 = lambda v: v.reshape(1, -1).astype(F32)
    ff1 = (row2(ff1_norm_pre[0]), ff1_w_gate[0].astype(BF16), ff1_w_up[0].astype(BF16), ff1_w_down[0].astype(BF16),
           row2(ff1_norm_post[0]))
    ff2 = (row2(ff2_norm_pre[0]), ff2_w_gate[0].astype(BF16), ff2_w_up[0].astype(BF16), ff2_w_down[0].astype(BF16),
           row2(ff2_norm_post[0]))
    w_in0 = w_in[0]
    wqkv_t = w_in0[:, :QKV_WIDTH].T.astype(BF16)
    dt_cols = w_in0[:, QKV_WIDTH + SSM_WIDTH + CONV_DIM:]
    wssm = jnp.concatenate([w_in0[:, QKV_WIDTH:QKV_WIDTH + SSM_WIDTH + CONV_DIM], dt_cols,
                            jnp.zeros((D_MODEL, DT_LANES - dt_cols.shape[1]), F32)], axis=1).astype(BF16)
    g_mix = row2(mix_norm_pre[0])
    gq = q_norm[0].reshape(HEAD_DIM, 1).astype(F32)
    gk = k_norm[0].reshape(HEAD_DIM, 1).astype(F32)
    n_dt = 2 * N_SSM_HEADS
    dtb = jnp.zeros((1, DT_LANES), F32).at[0, :n_dt].set(dt_bias[0].reshape(-1))
    alog = jnp.zeros((1, DT_LANES), F32).at[0, n_dt:2 * n_dt].set(a_log[0].reshape(-1))
    dskip = jnp.repeat(d_skip[0].astype(F32), HEAD_DIM).reshape(1, SSM_WIDTH)
    gnorm = row2(ssm_norm[0])
    cw = conv_w[0].astype(F32)
    cb = row2(conv_b[0])
    w_out_att = w_out[0][:ATT_WIDTH].astype(BF16)
    w_out_ssm = w_out[0][ATT_WIDTH:].astype(BF16)
    g_post = row2(mix_norm_post[0])

    def in_projection(h, cos_t, sin_t, tm):
        return _inproj(h, g_mix, wqkv_t, wssm, cos_t, sin_t, gq, gk, tm=tm)

    x_meta = jnp.concatenate([jnp.zeros((META_LEAD, D_MODEL), F32), meta_tokens.astype(F32)], axis=0)
    h_meta = _ffn(x_meta, *ff1, tm=META_ROWS)
    _, k_chunk, v_t_chunk, _, xbc_meta, dt_meta = in_projection(
        h_meta, jnp.ones((HEAD_DIM, META_ROWS), F32), jnp.zeros((HEAD_DIM, META_ROWS), F32), META_ROWS)
    k_meta = k_chunk[:, META_LEAD:, :]
    v_t_meta = jnp.pad(v_t_chunk[:, 0, :, META_LEAD:], ((0, 0), (0, 0), (0, META_LEAD)))

    def trunk(x):
        batch, seq, _ = x.shape
        n = batch * seq
        tm = _tile_of(n, ROW_TILE)
        tm_ff = _tile_of(n, FFN_ROW_TILE)
        h1 = _ffn(x.reshape(n, D_MODEL), *ff1, tm=tm_ff)
        cos_t, sin_t = _rope_tables_t(seq)
        tm_in = _tile_of(seq, ATT_TK)
        q_t, k, v_t, z, xbc, dt_raw = in_projection(h1, cos_t, sin_t, tm_in)
        o_att = _attention(q_t, k, v_t, k_meta, v_t_meta, batch=batch, seq=seq, tq=_tile_of(seq, ATT_TQ))
        o_ssm = _ssd(xbc, dt_raw, z, xbc_meta, dt_meta, cw, cb, dtb, alog, dskip, gnorm, batch=batch, seq=seq)
        h2 = _outproj(o_att, o_ssm, h1, w_out_att, w_out_ssm, g_post, tm=tm)
        return _ffn(h2, *ff2, tm=tm_ff).reshape(batch, seq, D_MODEL)

    return trunk(x_prompt), trunk(x_sample)
```

```python
import math

import jax
import jax.numpy as jnp
from jax import lax
from jax.experimental import pallas as pl
from jax.experimental.pallas import tpu as pltpu

F32 = jnp.float32
BF16 = jnp.bfloat16

D_MODEL = 2048
D_FF = 5632
N_META = 16
GRID_W = 64
EPS = 1e-6
HEAD_DIM = 64
ATT_WIDTH = 1024
N_Q_HEADS = 16
N_KV_HEADS = 4
Q_PER_KV = 4
ROPE_THETA = 10000.0
SSM_WIDTH = 1024
N_SSM_HEADS = 16
N_SSM_GROUPS = 2
HEADS_PER_GROUP = N_SSM_HEADS // N_SSM_GROUPS
D_STATE = 128
D_CONV = 5
CHUNK = 128
CONV_DIM = SSM_WIDTH + 2 * N_SSM_GROUPS * D_STATE
KV_WIDTH = N_KV_HEADS * HEAD_DIM
QKV_WIDTH = ATT_WIDTH + 2 * KV_WIDTH
DT_LANES = 128
SSM_PROJ = SSM_WIDTH + CONV_DIM + DT_LANES

META_ROWS = CHUNK
META_LEAD = CHUNK - N_META
QK_AUG = 128
V_AUG = 128
CONV_HALO = 8

FF_TILE = 512
FF_SUB = 2
FFN_ROW_TILE = 1024
FF_LAST_ROWS = 256
FF_ROWS = 512
ROW_TILE = 512
ATT_TQ = 512
ATT_TK = 512
ATT_RING = 4
ATT_UNROLL = 8
ATT_EMPTY = -1e30
VMEM_LIMIT = 56 * 1024 * 1024
FFN_VMEM_LIMIT = 60 * 1024 * 1024


def _cparams(sem, vmem_limit=VMEM_LIMIT):
    return pltpu.CompilerParams(dimension_semantics=sem, vmem_limit_bytes=vmem_limit)


def _rms(x, gain):
    ms = jnp.mean(x * x, axis=-1, keepdims=True)
    return x * lax.rsqrt(ms + EPS) * gain


def _ffn_body(x_ref, gpre_ref, wg_ref, wu_ref, wd_ref, gpost_ref, o_ref, u_ref):
    j = pl.program_id(1)

    @pl.when(j == 0)
    def _():
        u_ref[...] = _rms(x_ref[...], gpre_ref[...]).astype(BF16)
        o_ref[...] = jnp.zeros_like(o_ref)

    w = FF_TILE // FF_SUB
    tm = x_ref.shape[0]
    last = pl.num_programs(1) - 1

    def row_block(rows, finalize):
        u = u_ref[rows, :]
        gate_up = []
        for c in range(FF_SUB):
            gate_up.append((jnp.dot(u, wg_ref[:, c * w:(c + 1) * w], preferred_element_type=F32),
                            jnp.dot(u, wu_ref[:, c * w:(c + 1) * w], preferred_element_type=F32)))
        acc = o_ref[rows, :]
        for c, (g, up) in enumerate(gate_up):
            act = (g * jax.nn.sigmoid(g) * up).astype(BF16)
            acc = acc + jnp.dot(act, wd_ref[c * w:(c + 1) * w, :], preferred_element_type=F32)
        if finalize:
            acc = x_ref[rows, :] + 0.5 * _rms(acc, gpost_ref[...])
        o_ref[rows, :] = acc

    @pl.when(j < last)
    def _():
        blk = min(tm, FF_ROWS)
        for r in range(tm // blk):
            row_block(slice(r * blk, (r + 1) * blk), False)

    @pl.when(j == last)
    def _():
        blk = min(tm, FF_LAST_ROWS)
        for r in range(tm // blk):
            row_block(slice(r * blk, (r + 1) * blk), True)


def _ffn(x, gpre, wg, wu, wd, gpost, *, tm):
    n = x.shape[0]
    return pl.pallas_call(
        _ffn_body,
        grid=(n // tm, D_FF // FF_TILE),
        in_specs=[
            pl.BlockSpec((tm, D_MODEL), lambda i, j: (i, 0)),
            pl.BlockSpec((1, D_MODEL), lambda i, j: (0, 0)),
            pl.BlockSpec((D_MODEL, FF_TILE), lambda i, j: (0, j)),
            pl.BlockSpec((D_MODEL, FF_TILE), lambda i, j: (0, j)),
            pl.BlockSpec((FF_TILE, D_MODEL), lambda i, j: (j, 0)),
            pl.BlockSpec((1, D_MODEL), lambda i, j: (0, 0)),
        ],
        out_specs=pl.BlockSpec((tm, D_MODEL), lambda i, j: (i, 0)),
        out_shape=jax.ShapeDtypeStruct((n, D_MODEL), F32),
        scratch_shapes=[pltpu.VMEM((tm, D_MODEL), BF16)],
        compiler_params=_cparams(("parallel", "arbitrary"), FFN_VMEM_LIMIT),
        name="ffn",
    )(x, gpre, wg, wu, wd, gpost)


def _inproj_body(h_ref, g_ref, wqkv_t_ref, wssm_ref, cos_ref, sin_ref, gq_ref, gk_ref,
                 q_t_ref, k_ref, v_t_ref, z_ref, xbc_ref, dt_ref):
    tm = h_ref.shape[0]
    u = _rms(h_ref[...], g_ref[...]).astype(BF16)
    p_t = lax.dot_general(wqkv_t_ref[...], u, (((1,), (1,)), ((), ())), preferred_element_type=F32)
    ps = jnp.dot(u, wssm_ref[...], preferred_element_type=F32)
    z_ref[...] = ps[:, :SSM_WIDTH]
    xbc_ref[...] = ps[:, SSM_WIDTH:SSM_WIDTH + CONV_DIM]
    dt_ref[...] = ps[:, SSM_WIDTH + CONV_DIM:]

    cos_t = cos_ref[...]
    sin_t = sin_ref[...]

    def norm_rope(x, gain):
        ss = jnp.sum(x * x, axis=0, keepdims=True)
        xg = x * lax.rsqrt(ss * (1.0 / HEAD_DIM) + EPS) * gain
        xs = jnp.concatenate([xg[16:32], xg[0:16], xg[48:64], xg[32:48]], axis=0)
        return xg * cos_t + xs * sin_t

    q_tail = jnp.zeros((QK_AUG - HEAD_DIM, tm), BF16)
    k_tail = jnp.zeros((QK_AUG - HEAD_DIM, tm), F32)
    v_row = lax.broadcasted_iota(jnp.int32, (V_AUG - HEAD_DIM, tm), 0)
    v_tail = jnp.where(v_row == 0, 1.0, 0.0).astype(BF16)
    gq = gq_ref[...]
    gk = gk_ref[...]
    q_scale = HEAD_DIM ** -0.5 * math.log2(math.e)
    for h in range(N_Q_HEADS):
        q = norm_rope(p_t[HEAD_DIM * h:HEAD_DIM * (h + 1)], gq) * q_scale
        q_t_ref[h, 0:HEAD_DIM, :] = q.astype(BF16)
        q_t_ref[h, HEAD_DIM:QK_AUG, :] = q_tail
    for h in range(N_KV_HEADS):
        k = norm_rope(p_t[ATT_WIDTH + HEAD_DIM * h:ATT_WIDTH + HEAD_DIM * (h + 1)], gk)
        k_aug_t = jnp.concatenate([k, k_tail], axis=0)
        k_ref[h] = k_aug_t.T.astype(BF16)
        v0 = ATT_WIDTH + KV_WIDTH + HEAD_DIM * h
        v_t_ref[h, 0:HEAD_DIM, :] = p_t[v0:v0 + HEAD_DIM].astype(BF16)
        v_t_ref[h, HEAD_DIM:V_AUG, :] = v_tail


def _inproj(h, gain, wqkv_t, wssm, cos_t, sin_t, gq, gk, *, tm):
    n = h.shape[0]
    n_tab = cos_t.shape[1] // tm
    const = lambda i: (0, 0)
    return pl.pallas_call(
        _inproj_body,
        grid=(n // tm,),
        in_specs=[
            pl.BlockSpec((tm, D_MODEL), lambda i: (i, 0)),
            pl.BlockSpec((1, D_MODEL), const),
            pl.BlockSpec((QKV_WIDTH, D_MODEL), const, pipeline_mode=pl.Buffered(1)),
            pl.BlockSpec((D_MODEL, SSM_PROJ), const, pipeline_mode=pl.Buffered(1)),
            pl.BlockSpec((HEAD_DIM, tm), lambda i: (0, i % n_tab)),
            pl.BlockSpec((HEAD_DIM, tm), lambda i: (0, i % n_tab)),
            pl.BlockSpec((HEAD_DIM, 1), const),
            pl.BlockSpec((HEAD_DIM, 1), const),
        ],
        out_specs=[
            pl.BlockSpec((N_Q_HEADS, QK_AUG, tm), lambda i: (0, 0, i)),
            pl.BlockSpec((N_KV_HEADS, tm, QK_AUG), lambda i: (0, i, 0)),
            pl.BlockSpec((N_KV_HEADS, None, V_AUG, tm), lambda i: (0, i, 0, 0)),
            pl.BlockSpec((tm, SSM_WIDTH), lambda i: (i, 0)),
            pl.BlockSpec((tm, CONV_DIM), lambda i: (i, 0)),
            pl.BlockSpec((tm, DT_LANES), lambda i: (i, 0)),
        ],
        out_shape=[
            jax.ShapeDtypeStruct((N_Q_HEADS, QK_AUG, n), BF16),
            jax.ShapeDtypeStruct((N_KV_HEADS, n, QK_AUG), BF16),
            jax.ShapeDtypeStruct((N_KV_HEADS, n // tm, V_AUG, tm), BF16),
            jax.ShapeDtypeStruct((n, SSM_WIDTH), F32),
            jax.ShapeDtypeStruct((n, CONV_DIM), F32),
            jax.ShapeDtypeStruct((n, DT_LANES), F32),
        ],
        compiler_params=_cparams(("parallel",)),
        name="in_proj",
    )(h, gain, wqkv_t, wssm, cos_t, sin_t, gq, gk)


def _attn_body(zero_ref, q_t_ref, k_ref, v_t_ref, km_ref, vm_ref, o_ref, m_ref, acc_ref, *ring):
    s_bufs, ml_bufs = ring[:ATT_RING], ring[ATT_RING:]
    tq = q_t_ref.shape[2]
    nk, tk = v_t_ref.shape[1], v_t_ref.shape[3]
    lag = ATT_RING - 1
    rows = pl.ds(pl.multiple_of(zero_ref[0], 8), tk)

    def scores(j, h):
        k_tile = k_ref[h // Q_PER_KV, pl.ds(pl.multiple_of(j * tk, tk), tk), :]
        s = jnp.dot(k_tile, q_t_ref[h], preferred_element_type=F32)
        s_bufs[h % ATT_RING][rows, :] = s
        ml_bufs[h % ATT_RING][...] = jnp.max(s, axis=0, keepdims=True)

    def softmax_pv(j, h):
        m_old = m_ref[h]
        m_new = jnp.maximum(m_old, ml_bufs[h % ATT_RING][...])
        p = jnp.exp2(s_bufs[h % ATT_RING][rows, :] - m_new).astype(BF16)
        pv = jnp.dot(v_t_ref[h // Q_PER_KV, j], p, preferred_element_type=F32)
        acc_ref[h] = acc_ref[h] * jnp.exp2(m_old - m_new) + pv
        m_ref[h] = m_new

    p_pad = jnp.zeros((META_ROWS - N_META, tq), BF16)
    s_all = [jnp.dot(km_ref[h // Q_PER_KV], q_t_ref[h], preferred_element_type=F32) for h in range(N_Q_HEADS)]
    for h, s in enumerate(s_all):
        m = jnp.max(s, axis=0, keepdims=True)
        p = jnp.concatenate([jnp.exp2(s - m).astype(BF16), p_pad], axis=0)
        acc_ref[h] = jnp.dot(vm_ref[h // Q_PER_KV], p, preferred_element_type=F32)
        m_ref[h] = m

    for h in range(N_Q_HEADS - lag, N_Q_HEADS):
        s_bufs[h % ATT_RING][rows, :] = jnp.full((tk, tq), ATT_EMPTY, F32)
        ml_bufs[h % ATT_RING][...] = jnp.full((1, tq), ATT_EMPTY, F32)

    def key_tile(j, carry):
        jm1 = jnp.maximum(j - 1, 0)
        for h in range(N_Q_HEADS):
            d = h - lag
            scores(j, h)
            if d < 0:
                softmax_pv(jm1, d + N_Q_HEADS)
            else:
                softmax_pv(j, d)
        return carry

    lax.fori_loop(0, nk, key_tile, 0, unroll=math.gcd(nk, ATT_UNROLL))
    for h in range(N_Q_HEADS - lag, N_Q_HEADS):
        softmax_pv(nk - 1, h)

    for g in range(N_KV_HEADS):
        outs = []
        for hh in range(Q_PER_KV):
            acc = acc_ref[g * Q_PER_KV + hh]
            outs.append(acc[0:HEAD_DIM] / acc[HEAD_DIM:HEAD_DIM + 1])
        w = Q_PER_KV * HEAD_DIM
        o_ref[:, g * w:(g + 1) * w] = jnp.concatenate(outs, axis=0).T.astype(BF16)


def _attention(q_t, k, v_t, k_meta, v_t_meta, *, batch, seq, tq):
    n = batch * seq
    nq = seq // tq
    tk = v_t.shape[3]
    nk = seq // tk
    kv_bytes = N_KV_HEADS * seq * (QK_AUG + V_AUG) * 2
    kv_mode = dict(pipeline_mode=pl.Buffered(1)) if 2 * kv_bytes > VMEM_LIMIT // 2 else {}
    grid_spec = pltpu.PrefetchScalarGridSpec(
        num_scalar_prefetch=1,
        grid=(batch, nq),
        in_specs=[
            pl.BlockSpec((N_Q_HEADS, QK_AUG, tq), lambda b, i, z: (0, 0, b * nq + i)),
            pl.BlockSpec((N_KV_HEADS, seq, QK_AUG), lambda b, i, z: (0, b, 0), **kv_mode),
            pl.BlockSpec((N_KV_HEADS, nk, V_AUG, tk), lambda b, i, z: (0, b, 0, 0), **kv_mode),
            pl.BlockSpec((N_KV_HEADS, N_META, QK_AUG), lambda b, i, z: (0, 0, 0)),
            pl.BlockSpec((N_KV_HEADS, V_AUG, META_ROWS), lambda b, i, z: (0, 0, 0)),
        ],
        out_specs=pl.BlockSpec((tq, ATT_WIDTH), lambda b, i, z: (b * nq + i, 0)),
        scratch_shapes=[
            pltpu.VMEM((N_Q_HEADS, 1, tq), F32),
            pltpu.VMEM((N_Q_HEADS, V_AUG, tq), F32),
        ] + [pltpu.VMEM((tk, tq), F32)] * ATT_RING + [pltpu.VMEM((1, tq), F32)] * ATT_RING,
    )
    return pl.pallas_call(
        _attn_body,
        grid_spec=grid_spec,
        out_shape=jax.ShapeDtypeStruct((n, ATT_WIDTH), BF16),
        compiler_params=_cparams(("parallel", "arbitrary")),
        name="attention",
    )(jnp.zeros((1,), jnp.int32), q_t, k, v_t, k_meta, v_t_meta)


def _split3(x):
    hi = x.astype(BF16)
    r1 = x - hi.astype(F32)
    mid = r1.astype(BF16)
    lo = (r1 - mid.astype(F32)).astype(BF16)
    return hi, mid, lo


def _tri_dot(tri, x):
    hi, mid, lo = _split3(x)
    return (jnp.dot(tri, hi, preferred_element_type=F32) + jnp.dot(tri, mid, preferred_element_type=F32)
            + jnp.dot(tri, lo, preferred_element_type=F32))


def _tri_masks():
    r = lax.broadcasted_iota(jnp.int32, (CHUNK, CHUNK), 0)
    c = lax.broadcasted_iota(jnp.int32, (CHUNK, CHUNK), 1)
    return c <= r, c >= r


def _dt_comb(dt_raw, dtb_ref, alog_ref, valid_from=None):
    x = dt_raw + dtb_ref[...]
    sp = jnp.maximum(x, 0.0) + jnp.log(1.0 + jnp.exp(-jnp.abs(x)))
    lane = lax.broadcasted_iota(jnp.int32, sp.shape, 1)
    dtv = jnp.where(lane < 2 * N_SSM_HEADS, sp, 0.0)
    if valid_from is not None:
        row = lax.broadcasted_iota(jnp.int32, sp.shape, 0)
        dtv = jnp.where(row >= valid_from, dtv, 0.0)
    lane1 = lax.broadcasted_iota(jnp.int32, (1, DT_LANES), 1)
    a_mult = jnp.where((lane1 >= 2 * N_SSM_HEADS) & (lane1 < 4 * N_SSM_HEADS), -jnp.exp(alog_ref[...]), 0.0)
    return dtv + pltpu.roll(dtv, 2 * N_SSM_HEADS, axis=1) * a_mult


def _conv_silu(ext_ref, cw_ref, cb_ref):
    xe = ext_ref[...]
    n_ext = xe.shape[0]
    acc = cb_ref[...] + xe[CONV_HALO:CONV_HALO + CHUNK] * cw_ref[D_CONV // 2:D_CONV // 2 + 1, :]
    for j in range(D_CONV):
        if j != D_CONV // 2:
            tap = pltpu.roll(xe, (D_CONV // 2 - j) % n_ext, axis=0)[CONV_HALO:CONV_HALO + CHUNK]
            acc = acc + tap * cw_ref[j:j + 1, :]
    return acc * jax.nn.sigmoid(acc)


DT_F, DT_B = 0, N_SSM_HEADS
A_F, A_B = 2 * N_SSM_HEADS, 3 * N_SSM_HEADS


def _chunk_states(xs_t, comb_t, cum_t, b_mat, dt_row0, a_row0, end_col):
    dec = jnp.exp(cum_t[a_row0:a_row0 + N_SSM_HEADS, end_col:end_col + 1] - cum_t[a_row0:a_row0 + N_SSM_HEADS])
    out = []
    for g in range(N_SSM_GROUPS):
        rows = []
        for hh in range(HEADS_PER_GROUP):
            h = g * HEADS_PER_GROUP + hh
            xr = xs_t[HEAD_DIM * h:HEAD_DIM * (h + 1)] * (comb_t[dt_row0 + h:dt_row0 + h + 1] * dec[h:h + 1])
            rows.append(xr.astype(BF16))
        xrd = jnp.concatenate(rows, axis=0)
        out.append(jnp.dot(xrd, b_mat[:, D_STATE * g:D_STATE * (g + 1)].astype(BF16), preferred_element_type=F32))
    return out


def _update_state(state_ref, contrib, cum_t, a_row0, end_col):
    cd = jnp.exp(cum_t[a_row0:a_row0 + N_SSM_HEADS, end_col:end_col + 1])
    for g in range(N_SSM_GROUPS):
        for hh in range(HEADS_PER_GROUP):
            h = g * HEADS_PER_GROUP + hh
            sl = slice(HEAD_DIM * h, HEAD_DIM * (h + 1))
            state_ref[sl, :] = state_ref[sl, :] * cd[h:h + 1] + contrib[g][HEAD_DIM * hh:HEAD_DIM * (hh + 1)]


def _ssd_fwd_body(xc_ref, xp_ref, xn_ref, dt_ref, xm_ref, dtm_ref, cw_ref, cb_ref, dtb_ref, alog_ref, dskip_ref,
                  ypart_ref, xact_ref, state_ref, ext_ref):
    t = pl.program_id(1)
    ltri, utri = _tri_masks()
    ltri_b = jnp.where(ltri, 1.0, 0.0).astype(BF16)
    utri_b = jnp.where(utri, 1.0, 0.0).astype(BF16)

    @pl.when(t == 0)
    def _():
        ext_ref[0:CONV_HALO, :] = jnp.zeros((CONV_HALO, CONV_DIM), F32)
        ext_ref[CONV_HALO:CONV_HALO + CHUNK, :] = xm_ref[...]
        ext_ref[CONV_HALO + CHUNK:, :] = xc_ref[0:CONV_HALO, :]
        xact = _conv_silu(ext_ref, cw_ref, cb_ref)
        comb = _dt_comb(dtm_ref[...], dtb_ref, alog_ref, valid_from=META_LEAD)
        pc_t = _tri_dot(ltri_b, comb).T
        contrib = _chunk_states(xact[:, :SSM_WIDTH].T, comb.T, pc_t, xact[:, SSM_WIDTH:SSM_WIDTH + N_SSM_GROUPS * D_STATE],
                                DT_F, A_F, CHUNK - 1)
        for g in range(N_SSM_GROUPS):
            state_ref[HEADS_PER_GROUP * HEAD_DIM * g:HEADS_PER_GROUP * HEAD_DIM * (g + 1), :] = contrib[g]

    ext_ref[0:CONV_HALO, :] = jnp.where(t == 0, xm_ref[CHUNK - CONV_HALO:, :], xp_ref[...])
    ext_ref[CONV_HALO:CONV_HALO + CHUNK, :] = xc_ref[...]
    ext_ref[CONV_HALO + CHUNK:, :] = jnp.where(t == pl.num_programs(1) - 1, 0.0, xn_ref[...])
    xact = _conv_silu(ext_ref, cw_ref, cb_ref)
    xact_ref[...] = xact
    xs = xact[:, :SSM_WIDTH]
    b_mat = xact[:, SSM_WIDTH:SSM_WIDTH + N_SSM_GROUPS * D_STATE]
    c_mat = xact[:, SSM_WIDTH + N_SSM_GROUPS * D_STATE:]

    comb = _dt_comb(dt_ref[...], dtb_ref, alog_ref)
    pc = _tri_dot(ltri_b, comb)
    sc = _tri_dot(utri_b, comb)
    pc_t, sc_t, comb_t = pc.T, sc.T, comb.T
    xs_t = xs.T
    exp_p = jnp.exp(pc_t[A_F:A_F + N_SSM_HEADS])

    y_rows = []
    for g in range(N_SSM_GROUPS):
        bg = b_mat[:, D_STATE * g:D_STATE * (g + 1)].astype(BF16)
        cg_t = c_mat[:, D_STATE * g:D_STATE * (g + 1)].T.astype(BF16)
        cb_t = jnp.dot(bg, cg_t, preferred_element_type=F32)
        cb_f = jnp.where(utri, cb_t, 0.0)
        cb_b = jnp.where(ltri, cb_t, 0.0)
        st = state_ref[HEADS_PER_GROUP * HEAD_DIM * g:HEADS_PER_GROUP * HEAD_DIM * (g + 1), :].astype(BF16)
        yo_t = jnp.dot(st, cg_t, preferred_element_type=F32)
        for hh in range(HEADS_PER_GROUP):
            h = g * HEADS_PER_GROUP + hh
            ef = jnp.exp(jnp.minimum(pc_t[A_F + h:A_F + h + 1] - pc[:, A_F + h:A_F + h + 1], 0.0))
            eb = jnp.exp(jnp.minimum(sc_t[A_B + h:A_B + h + 1] - sc[:, A_B + h:A_B + h + 1], 0.0))
            m_t = jnp.concatenate([(cb_f * ef).astype(BF16), (cb_b * eb).astype(BF16)], axis=0)
            x_h = xs_t[HEAD_DIM * h:HEAD_DIM * (h + 1)]
            xr = jnp.concatenate([(x_h * comb_t[DT_F + h:DT_F + h + 1]).astype(BF16),
                                  (x_h * comb_t[DT_B + h:DT_B + h + 1]).astype(BF16)], axis=1)
            y_h = jnp.dot(xr, m_t, preferred_element_type=F32)
            y_rows.append(y_h + yo_t[HEAD_DIM * hh:HEAD_DIM * (hh + 1)] * exp_p[h:h + 1])
    y_t = jnp.concatenate(y_rows, axis=0)
    ypart_ref[...] = y_t.T + xs * dskip_ref[...]

    contrib = _chunk_states(xs_t, comb_t, pc_t, b_mat, DT_F, A_F, CHUNK - 1)
    _update_state(state_ref, contrib, pc_t, A_F, CHUNK - 1)


def _ssd_bwd_body(xact_ref, dt_ref, ypart_ref, z_ref, dtb_ref, alog_ref, gnorm_ref, o_ref, state_ref):
    t = pl.program_id(1)
    _, utri = _tri_masks()
    utri_b = jnp.where(utri, 1.0, 0.0).astype(BF16)

    @pl.when(t == 0)
    def _():
        state_ref[...] = jnp.zeros_like(state_ref)

    xact = xact_ref[...]
    xs = xact[:, :SSM_WIDTH]
    b_mat = xact[:, SSM_WIDTH:SSM_WIDTH + N_SSM_GROUPS * D_STATE]
    c_mat = xact[:, SSM_WIDTH + N_SSM_GROUPS * D_STATE:]
    comb = _dt_comb(dt_ref[...], dtb_ref, alog_ref)
    sc_t = _tri_dot(utri_b, comb).T
    comb_t = comb.T
    exp_s = jnp.exp(sc_t[A_B:A_B + N_SSM_HEADS])

    y_rows = []
    for g in range(N_SSM_GROUPS):
        cg_t = c_mat[:, D_STATE * g:D_STATE * (g + 1)].T.astype(BF16)
        st = state_ref[HEADS_PER_GROUP * HEAD_DIM * g:HEADS_PER_GROUP * HEAD_DIM * (g + 1), :].astype(BF16)
        yo_t = jnp.dot(st, cg_t, preferred_element_type=F32)
        for hh in range(HEADS_PER_GROUP):
            h = g * HEADS_PER_GROUP + hh
            y_rows.append(yo_t[HEAD_DIM * hh:HEAD_DIM * (hh + 1)] * exp_s[h:h + 1])
    y = ypart_ref[...] + jnp.concatenate(y_rows, axis=0).T
    z = z_ref[...]
    y = y * (z * jax.nn.sigmoid(z))
    gw = SSM_WIDTH // N_SSM_GROUPS
    for g in range(N_SSM_GROUPS):
        o_ref[:, gw * g:gw * (g + 1)] = _rms(y[:, gw * g:gw * (g + 1)], gnorm_ref[:, gw * g:gw * (g + 1)]).astype(BF16)

    contrib = _chunk_states(xs.T, comb_t, sc_t, b_mat, DT_B, A_B, 0)
    _update_state(state_ref, contrib, sc_t, A_B, 0)


def _ssd(xbc, dt_raw, z, xbc_meta, dt_meta, conv_w, conv_b, dtb, alog, dskip, gnorm, *, batch, seq):
    n = batch * seq
    nc = seq // CHUNK
    hb = CHUNK // CONV_HALO
    const = lambda b, t: (0, 0)
    ypart, xact = pl.pallas_call(
        _ssd_fwd_body,
        grid=(batch, nc),
        in_specs=[
            pl.BlockSpec((CHUNK, CONV_DIM), lambda b, t: (b * nc + t, 0)),
            pl.BlockSpec((CONV_HALO, CONV_DIM), lambda b, t: (jnp.maximum((b * nc + t) * hb - 1, 0), 0)),
            pl.BlockSpec((CONV_HALO, CONV_DIM), lambda b, t: (jnp.minimum((b * nc + t + 1) * hb, n // CONV_HALO - 1), 0)),
            pl.BlockSpec((CHUNK, DT_LANES), lambda b, t: (b * nc + t, 0)),
            pl.BlockSpec((META_ROWS, CONV_DIM), const),
            pl.BlockSpec((META_ROWS, DT_LANES), const),
            pl.BlockSpec((D_CONV, CONV_DIM), const),
            pl.BlockSpec((1, CONV_DIM), const),
            pl.BlockSpec((1, DT_LANES), const),
            pl.BlockSpec((1, DT_LANES), const),
            pl.BlockSpec((1, SSM_WIDTH), const),
        ],
        out_specs=[
            pl.BlockSpec((CHUNK, SSM_WIDTH), lambda b, t: (b * nc + t, 0)),
            pl.BlockSpec((CHUNK, CONV_DIM), lambda b, t: (b * nc + t, 0)),
        ],
        out_shape=[
            jax.ShapeDtypeStruct((n, SSM_WIDTH), F32),
            jax.ShapeDtypeStruct((n, CONV_DIM), F32),
        ],
        scratch_shapes=[
            pltpu.VMEM((N_SSM_HEADS * HEAD_DIM, D_STATE), F32),
            pltpu.VMEM((CHUNK + 2 * CONV_HALO, CONV_DIM), F32),
        ],
        compiler_params=_cparams(("parallel", "arbitrary")),
        name="ssd_fwd",
    )(xbc, xbc, xbc, dt_raw, xbc_meta, dt_meta, conv_w, conv_b, dtb, alog, dskip)

    rev = lambda b, t: (b * nc + nc - 1 - t, 0)
    return pl.pallas_call(
        _ssd_bwd_body,
        grid=(batch, nc),
        in_specs=[
            pl.BlockSpec((CHUNK, CONV_DIM), rev),
            pl.BlockSpec((CHUNK, DT_LANES), rev),
            pl.BlockSpec((CHUNK, SSM_WIDTH), rev),
            pl.BlockSpec((CHUNK, SSM_WIDTH), rev),
            pl.BlockSpec((1, DT_LANES), const),
            pl.BlockSpec((1, DT_LANES), const),
            pl.BlockSpec((1, SSM_WIDTH), const),
        ],
        out_specs=pl.BlockSpec((CHUNK, SSM_WIDTH), rev),
        out_shape=jax.ShapeDtypeStruct((n, SSM_WIDTH), BF16),
        scratch_shapes=[pltpu.VMEM((N_SSM_HEADS * HEAD_DIM, D_STATE), F32)],
        compiler_params=_cparams(("parallel", "arbitrary")),
        name="ssd_bwd",
    )(xact, dt_raw, ypart, z, dtb, alog, gnorm)


def _outproj_body(oa_ref, os_ref, h_ref, wa_ref, ws_ref, g_ref, o_ref):
    mix = (jnp.dot(oa_ref[...], wa_ref[...], preferred_element_type=F32)
           + jnp.dot(os_ref[...], ws_ref[...], preferred_element_type=F32))
    o_ref[...] = h_ref[...] + _rms(mix, g_ref[...])


def _outproj(o_att, o_ssm, h, w_att, w_ssm, gain, *, tm):
    n = h.shape[0]
    const = lambda i: (0, 0)
    return pl.pallas_call(
        _outproj_body,
        grid=(n // tm,),
        in_specs=[
            pl.BlockSpec((tm, ATT_WIDTH), lambda i: (i, 0)),
            pl.BlockSpec((tm, SSM_WIDTH), lambda i: (i, 0)),
            pl.BlockSpec((tm, D_MODEL), lambda i: (i, 0)),
            pl.BlockSpec((ATT_WIDTH, D_MODEL), const),
            pl.BlockSpec((SSM_WIDTH, D_MODEL), const),
            pl.BlockSpec((1, D_MODEL), const),
        ],
        out_specs=pl.BlockSpec((tm, D_MODEL), lambda i: (i, 0)),
        out_shape=jax.ShapeDtypeStruct((n, D_MODEL), F32),
        compiler_params=_cparams(("parallel",)),
        name="out_proj",
    )(o_att, o_ssm, h, w_att, w_ssm, gain)


def _rope_tables_t(n_tok):
    pos = jnp.arange(n_tok)
    row = (pos // GRID_W).astype(F32)
    col = (pos % GRID_W).astype(F32)
    n_freq = HEAD_DIM // 4
    inv_freq = ROPE_THETA ** (-jnp.arange(n_freq, dtype=F32) / n_freq)
    ang_r = inv_freq[:, None] * row[None, :]
    ang_c = inv_freq[:, None] * col[None, :]
    ang = jnp.concatenate([ang_r, ang_r, ang_c, ang_c], axis=0)
    sign = jnp.concatenate([-jnp.ones((n_freq, 1), F32), jnp.ones((n_freq, 1), F32)] * 2, axis=0)
    return jnp.cos(ang), jnp.sin(ang) * sign


def _tile_of(n, pref):
    return pref if n % pref == 0 else n


def kernel(x_prompt, x_sample, meta_tokens, ff1_norm_pre, ff1_w_gate, ff1_w_up, ff1_w_down, ff1_norm_post,
           mix_norm_pre, w_in, conv_w, conv_b, a_log, dt_bias, d_skip, q_norm, k_norm, ssm_norm, w_out,
           mix_norm_post, ff2_norm_pre, ff2_w_gate, ff2_w_up, ff2_w_down, ff2_norm_post):
    assert ff1_w_gate.shape[0] == 1, "single-layer trunk"
    row2 = lambda v: v.reshape(1, -1).astype(F32)
    ff1 = (row2(ff1_norm_pre[0]), ff1_w_gate[0].astype(BF16), ff1_w_up[0].astype(BF16), ff1_w_down[0].astype(BF16),
           row2(ff1_norm_post[0]))
    ff2 = (row2(ff2_norm_pre[0]), ff2_w_gate[0].astype(BF16), ff2_w_up[0].astype(BF16), ff2_w_down[0].astype(BF16),
           row2(ff2_norm_post[0]))
    w_in0 = w_in[0]
    wqkv_t = w_in0[:, :QKV_WIDTH].T.astype(BF16)
    dt_cols = w_in0[:, QKV_WIDTH + SSM_WIDTH + CONV_DIM:]
    wssm = jnp.concatenate([w_in0[:, QKV_WIDTH:QKV_WIDTH + SSM_WIDTH + CONV_DIM], dt_cols,
                            jnp.zeros((D_MODEL, DT_LANES - dt_cols.shape[1]), F32)], axis=1).astype(BF16)
    g_mix = row2(mix_norm_pre[0])
    gq = q_norm[0].reshape(HEAD_DIM, 1).astype(F32)
    gk = k_norm[0].reshape(HEAD_DIM, 1).astype(F32)
    n_dt = 2 * N_SSM_HEADS
    dtb = jnp.zeros((1, DT_LANES), F32).at[0, :n_dt].set(dt_bias[0].reshape(-1))
    alog = jnp.zeros((1, DT_LANES), F32).at[0, n_dt:2 * n_dt].set(a_log[0].reshape(-1))
    dskip = jnp.repeat(d_skip[0].astype(F32), HEAD_DIM).reshape(1, SSM_WIDTH)
    gnorm = row2(ssm_norm[0])
    cw = conv_w[0].astype(F32)
    cb = row2(conv_b[0])
    w_out_att = w_out[0][:ATT_WIDTH].astype(BF16)
    w_out_ssm = w_out[0][ATT_WIDTH:].astype(BF16)
    g_post = row2(mix_norm_post[0])

    def in_projection(h, cos_t, sin_t, tm):
        return _inproj(h, g_mix, wqkv_t, wssm, cos_t, sin_t, gq, gk, tm=tm)

    x_meta = jnp.concatenate([jnp.zeros((META_LEAD, D_MODEL), F32), meta_tokens.astype(F32)], axis=0)
    h_meta = _ffn(x_meta, *ff1, tm=META_ROWS)
    _, k_chunk, v_t_chunk, _, xbc_meta, dt_meta = in_projection(
        h_meta, jnp.ones((HEAD_DIM, META_ROWS), F32), jnp.zeros((HEAD_DIM, META_ROWS), F32), META_ROWS)
    k_meta = k_chunk[:, META_LEAD:, :]
    v_t_meta = jnp.pad(v_t_chunk[:, 0, :, META_LEAD:], ((0, 0), (0, 0), (0, META_LEAD)))

    def trunk(x):
        batch, seq, _ = x.shape
        n = batch * seq
        tm = _tile_of(n, ROW_TILE)
        tm_ff = _tile_of(n, FFN_ROW_TILE)
        h1 = _ffn(x.reshape(n, D_MODEL), *ff1, tm=tm_ff)
        cos_t, sin_t = _rope_tables_t(seq)
        tm_in = _tile_of(seq, ATT_TK)
        q_t, k, v_t, z, xbc, dt_raw = in_projection(h1, cos_t, sin_t, tm_in)
        o_att = _attention(q_t, k, v_t, k_meta, v_t_meta, batch=batch, seq=seq, tq=_tile_of(seq, ATT_TQ))
        o_ssm = _ssd(xbc, dt_raw, z, xbc_meta, dt_meta, cw, cb, dtb, alog, dskip, gnorm, batch=batch, seq=seq)
        h2 = _outproj(o_att, o_ssm, h1, w_out_att, w_out_ssm, g_post, tm=tm)
        return _ffn(h2, *ff2, tm=tm_ff).reshape(batch, seq, D_MODEL)

    return trunk(x_prompt), trunk(x_sample)
```

```python
import math

import jax
import jax.numpy as jnp
from jax import lax
from jax.experimental import pallas as pl
from jax.experimental.pallas import tpu as pltpu

F32 = jnp.float32
BF16 = jnp.bfloat16

D_MODEL = 2048
D_FF = 5632
N_META = 16
GRID_W = 64
EPS = 1e-6
HEAD_DIM = 64
ATT_WIDTH = 1024
N_Q_HEADS = 16
N_KV_HEADS = 4
Q_PER_KV = 4
ROPE_THETA = 10000.0
SSM_WIDTH = 1024
N_SSM_HEADS = 16
N_SSM_GROUPS = 2
HEADS_PER_GROUP = N_SSM_HEADS // N_SSM_GROUPS
D_STATE = 128
D_CONV = 5
CHUNK = 128
CONV_DIM = SSM_WIDTH + 2 * N_SSM_GROUPS * D_STATE
KV_WIDTH = N_KV_HEADS * HEAD_DIM
QKV_WIDTH = ATT_WIDTH + 2 * KV_WIDTH
DT_LANES = 128
SSM_PROJ = SSM_WIDTH + CONV_DIM + DT_LANES

META_ROWS = CHUNK
META_LEAD = CHUNK - N_META
QK_AUG = 128
V_AUG = 128
CONV_HALO = 8

FF_TILE = 512
FF_SUB = 2
FFN_ROW_TILE = 1024
FF_LAST_ROWS = 256
FF_ROWS = 512
ROW_TILE = 512
ATT_TQ = 512
ATT_TK = 512
ATT_RING = 4
ATT_UNROLL = 8
ATT_EMPTY = -1e30
VMEM_LIMIT = 56 * 1024 * 1024
FFN_VMEM_LIMIT = 60 * 1024 * 1024


def _cparams(sem, vmem_limit=VMEM_LIMIT):
    return pltpu.CompilerParams(dimension_semantics=sem, vmem_limit_bytes=vmem_limit)


def _rms(x, gain):
    ms = jnp.mean(x * x, axis=-1, keepdims=True)
    return x * lax.rsqrt(ms + EPS) * gain


def _ffn_body(x_ref, gpre_ref, wg_ref, wu_ref, wd_ref, gpost_ref, o_ref, u_ref):
    j = pl.program_id(1)

    w = FF_TILE // FF_SUB
    tm = x_ref.shape[0]
    last = pl.num_programs(1) - 1

    def row_block(rows, first, finalize):
        if first:
            u = _rms(x_ref[rows, :], gpre_ref[...]).astype(BF16)
            u_ref[rows, :] = u
        else:
            u = u_ref[rows, :]
        gate_up = []
        for c in range(FF_SUB):
            gate_up.append((jnp.dot(u, wg_ref[:, c * w:(c + 1) * w], preferred_element_type=F32),
                            jnp.dot(u, wu_ref[:, c * w:(c + 1) * w], preferred_element_type=F32)))
        acc = None if first else o_ref[rows, :]
        for c, (g, up) in enumerate(gate_up):
            act = (g * jax.nn.sigmoid(g) * up).astype(BF16)
            d = jnp.dot(act, wd_ref[c * w:(c + 1) * w, :], preferred_element_type=F32)
            acc = d if acc is None else acc + d
        if finalize:
            acc = x_ref[rows, :] + 0.5 * _rms(acc, gpost_ref[...])
        o_ref[rows, :] = acc

    def step(blk, first, finalize):
        blk = min(tm, blk)
        for r in range(tm // blk):
            row_block(slice(r * blk, (r + 1) * blk), first, finalize)

    pl.when(j == 0)(lambda: step(FF_LAST_ROWS, True, False))
    pl.when((j > 0) & (j < last))(lambda: step(FF_ROWS, False, False))
    pl.when(j == last)(lambda: step(FF_LAST_ROWS, False, True))


def _ffn(x, gpre, wg, wu, wd, gpost, *, tm):
    n = x.shape[0]
    return pl.pallas_call(
        _ffn_body,
        grid=(n // tm, D_FF // FF_TILE),
        in_specs=[
            pl.BlockSpec((tm, D_MODEL), lambda i, j: (i, 0)),
            pl.BlockSpec((1, D_MODEL), lambda i, j: (0, 0)),
            pl.BlockSpec((D_MODEL, FF_TILE), lambda i, j: (0, j)),
            pl.BlockSpec((D_MODEL, FF_TILE), lambda i, j: (0, j)),
            pl.BlockSpec((FF_TILE, D_MODEL), lambda i, j: (j, 0)),
            pl.BlockSpec((1, D_MODEL), lambda i, j: (0, 0)),
        ],
        out_specs=pl.BlockSpec((tm, D_MODEL), lambda i, j: (i, 0)),
        out_shape=jax.ShapeDtypeStruct((n, D_MODEL), F32),
        scratch_shapes=[pltpu.VMEM((tm, D_MODEL), BF16)],
        compiler_params=_cparams(("parallel", "arbitrary"), FFN_VMEM_LIMIT),
        name="ffn",
    )(x, gpre, wg, wu, wd, gpost)


def _inproj_body(h_ref, g_ref, wqkv_t_ref, wssm_ref, cos_ref, sin_ref, gq_ref, gk_ref,
                 q_t_ref, k_ref, v_t_ref, z_ref, xbc_ref, dt_ref):
    tm = h_ref.shape[0]
    u = _rms(h_ref[...], g_ref[...]).astype(BF16)
    p_t = lax.dot_general(wqkv_t_ref[...], u, (((1,), (1,)), ((), ())), preferred_element_type=F32)
    ps = jnp.dot(u, wssm_ref[...], preferred_element_type=F32)
    z_ref[...] = ps[:, :SSM_WIDTH]
    xbc_ref[...] = ps[:, SSM_WIDTH:SSM_WIDTH + CONV_DIM]
    dt_ref[...] = ps[:, SSM_WIDTH + CONV_DIM:]

    cos_t = cos_ref[...]
    sin_t = sin_ref[...]

    def norm_rope(x, gain):
        ss = jnp.sum(x * x, axis=0, keepdims=True)
        xg = x * lax.rsqrt(ss * (1.0 / HEAD_DIM) + EPS) * gain
        xs = jnp.concatenate([xg[16:32], xg[0:16], xg[48:64], xg[32:48]], axis=0)
        return xg * cos_t + xs * sin_t

    q_tail = jnp.zeros((QK_AUG - HEAD_DIM, tm), BF16)
    k_tail = jnp.zeros((QK_AUG - HEAD_DIM, tm), F32)
    v_row = lax.broadcasted_iota(jnp.int32, (V_AUG - HEAD_DIM, tm), 0)
    v_tail = jnp.where(v_row == 0, 1.0, 0.0).astype(BF16)
    gq = gq_ref[...]
    gk = gk_ref[...]
    q_scale = HEAD_DIM ** -0.5 * math.log2(math.e)
    for h in range(N_Q_HEADS):
        q = norm_rope(p_t[HEAD_DIM * h:HEAD_DIM * (h + 1)], gq) * q_scale
        q_t_ref[h, 0:HEAD_DIM, :] = q.astype(BF16)
        q_t_ref[h, HEAD_DIM:QK_AUG, :] = q_tail
    for h in range(N_KV_HEADS):
        k = norm_rope(p_t[ATT_WIDTH + HEAD_DIM * h:ATT_WIDTH + HEAD_DIM * (h + 1)], gk)
        k_aug_t = jnp.concatenate([k, k_tail], axis=0)
        k_ref[h] = k_aug_t.T.astype(BF16)
        v0 = ATT_WIDTH + KV_WIDTH + HEAD_DIM * h
        v_t_ref[h, 0:HEAD_DIM, :] = p_t[v0:v0 + HEAD_DIM].astype(BF16)
        v_t_ref[h, HEAD_DIM:V_AUG, :] = v_tail


def _inproj(h, gain, wqkv_t, wssm, cos_t, sin_t, gq, gk, *, tm):
    n = h.shape[0]
    n_tab = cos_t.shape[1] // tm
    const = lambda i: (0, 0)
    return pl.pallas_call(
        _inproj_body,
        grid=(n // tm,),
        in_specs=[
            pl.BlockSpec((tm, D_MODEL), lambda i: (i, 0)),
            pl.BlockSpec((1, D_MODEL), const),
            pl.BlockSpec((QKV_WIDTH, D_MODEL), const, pipeline_mode=pl.Buffered(1)),
            pl.BlockSpec((D_MODEL, SSM_PROJ), const, pipeline_mode=pl.Buffered(1)),
            pl.BlockSpec((HEAD_DIM, tm), lambda i: (0, i % n_tab)),
            pl.BlockSpec((HEAD_DIM, tm), lambda i: (0, i % n_tab)),
            pl.BlockSpec((HEAD_DIM, 1), const),
            pl.BlockSpec((HEAD_DIM, 1), const),
        ],
        out_specs=[
            pl.BlockSpec((N_Q_HEADS, QK_AUG, tm), lambda i: (0, 0, i)),
            pl.BlockSpec((N_KV_HEADS, tm, QK_AUG), lambda i: (0, i, 0)),
            pl.BlockSpec((N_KV_HEADS, None, V_AUG, tm), lambda i: (0, i, 0, 0)),
            pl.BlockSpec((tm, SSM_WIDTH), lambda i: (i, 0)),
            pl.BlockSpec((tm, CONV_DIM), lambda i: (i, 0)),
            pl.BlockSpec((tm, DT_LANES), lambda i: (i, 0)),
        ],
        out_shape=[
            jax.ShapeDtypeStruct((N_Q_HEADS, QK_AUG, n), BF16),
            jax.ShapeDtypeStruct((N_KV_HEADS, n, QK_AUG), BF16),
            jax.ShapeDtypeStruct((N_KV_HEADS, n // tm, V_AUG, tm), BF16),
            jax.ShapeDtypeStruct((n, SSM_WIDTH), F32),
            jax.ShapeDtypeStruct((n, CONV_DIM), F32),
            jax.ShapeDtypeStruct((n, DT_LANES), F32),
        ],
        compiler_params=_cparams(("parallel",)),
        name="in_proj",
    )(h, gain, wqkv_t, wssm, cos_t, sin_t, gq, gk)


def _attn_body(zero_ref, q_t_ref, k_ref, v_t_ref, km_ref, vm_ref, o_ref, m_ref, acc_ref, *ring):
    s_bufs, ml_bufs = ring[:ATT_RING], ring[ATT_RING:]
    tq = q_t_ref.shape[2]
    nk, tk = v_t_ref.shape[1], v_t_ref.shape[3]
    lag = ATT_RING - 1
    rows = pl.ds(pl.multiple_of(zero_ref[0], 8), tk)

    def scores(j, h):
        k_tile = k_ref[h // Q_PER_KV, pl.ds(pl.multiple_of(j * tk, tk), tk), :]
        s = jnp.dot(k_tile, q_t_ref[h], preferred_element_type=F32)
        s_bufs[h % ATT_RING][rows, :] = s
        ml_bufs[h % ATT_RING][...] = jnp.max(s, axis=0, keepdims=True)

    def softmax_pv(j, h):
        m_old = m_ref[h]
        m_new = jnp.maximum(m_old, ml_bufs[h % ATT_RING][...])
        p = jnp.exp2(s_bufs[h % ATT_RING][rows, :] - m_new).astype(BF16)
        pv = jnp.dot(v_t_ref[h // Q_PER_KV, j], p, preferred_element_type=F32)
        acc_ref[h] = acc_ref[h] * jnp.exp2(m_old - m_new) + pv
        m_ref[h] = m_new

    p_pad = jnp.zeros((META_ROWS - N_META, tq), BF16)
    s_all = [jnp.dot(km_ref[h // Q_PER_KV], q_t_ref[h], preferred_element_type=F32) for h in range(N_Q_HEADS)]
    for h, s in enumerate(s_all):
        m = jnp.max(s, axis=0, keepdims=True)
        p = jnp.concatenate([jnp.exp2(s - m).astype(BF16), p_pad], axis=0)
        acc_ref[h] = jnp.dot(vm_ref[h // Q_PER_KV], p, preferred_element_type=F32)
        m_ref[h] = m

    for h in range(N_Q_HEADS - lag, N_Q_HEADS):
        s_bufs[h % ATT_RING][rows, :] = jnp.full((tk, tq), ATT_EMPTY, F32)
        ml_bufs[h % ATT_RING][...] = jnp.full((1, tq), ATT_EMPTY, F32)

    def key_tile(j, carry):
        jm1 = jnp.maximum(j - 1, 0)
        for h in range(N_Q_HEADS):
            d = h - lag
            scores(j, h)
            if d < 0:
                softmax_pv(jm1, d + N_Q_HEADS)
            else:
                softmax_pv(j, d)
        return carry

    lax.fori_loop(0, nk, key_tile, 0, unroll=math.gcd(nk, ATT_UNROLL))
    for h in range(N_Q_HEADS - lag, N_Q_HEADS):
        softmax_pv(nk - 1, h)

    for g in range(N_KV_HEADS):
        outs = []
        for hh in range(Q_PER_KV):
            acc = acc_ref[g * Q_PER_KV + hh]
            outs.append(acc[0:HEAD_DIM] / acc[HEAD_DIM:HEAD_DIM + 1])
        w = Q_PER_KV * HEAD_DIM
        o_ref[:, g * w:(g + 1) * w] = jnp.concatenate(outs, axis=0).T.astype(BF16)


def _attention(q_t, k, v_t, k_meta, v_t_meta, *, batch, seq, tq):
    n = batch * seq
    nq = seq // tq
    tk = v_t.shape[3]
    nk = seq // tk
    kv_bytes = N_KV_HEADS * seq * (QK_AUG + V_AUG) * 2
    kv_mode = dict(pipeline_mode=pl.Buffered(1)) if 2 * kv_bytes > VMEM_LIMIT // 2 else {}
    grid_spec = pltpu.PrefetchScalarGridSpec(
        num_scalar_prefetch=1,
        grid=(batch, nq),
        in_specs=[
            pl.BlockSpec((N_Q_HEADS, QK_AUG, tq), lambda b, i, z: (0, 0, b * nq + i)),
            pl.BlockSpec((N_KV_HEADS, seq, QK_AUG), lambda b, i, z: (0, b, 0), **kv_mode),
            pl.BlockSpec((N_KV_HEADS, nk, V_AUG, tk), lambda b, i, z: (0, b, 0, 0), **kv_mode),
            pl.BlockSpec((N_KV_HEADS, N_META, QK_AUG), lambda b, i, z: (0, 0, 0)),
            pl.BlockSpec((N_KV_HEADS, V_AUG, META_ROWS), lambda b, i, z: (0, 0, 0)),
        ],
        out_specs=pl.BlockSpec((tq, ATT_WIDTH), lambda b, i, z: (b * nq + i, 0)),
        scratch_shapes=[
            pltpu.VMEM((N_Q_HEADS, 1, tq), F32),
            pltpu.VMEM((N_Q_HEADS, V_AUG, tq), F32),
        ] + [pltpu.VMEM((tk, tq), F32)] * ATT_RING + [pltpu.VMEM((1, tq), F32)] * ATT_RING,
    )
    return pl.pallas_call(
        _attn_body,
        grid_spec=grid_spec,
        out_shape=jax.ShapeDtypeStruct((n, ATT_WIDTH), BF16),
        compiler_params=_cparams(("parallel", "arbitrary")),
        name="attention",
    )(jnp.zeros((1,), jnp.int32), q_t, k, v_t, k_meta, v_t_meta)


def _split3(x):
    hi = x.astype(BF16)
    r1 = x - hi.astype(F32)
    mid = r1.astype(BF16)
    lo = (r1 - mid.astype(F32)).astype(BF16)
    return hi, mid, lo


def _tri_dot(tri, x):
    hi, mid, lo = _split3(x)
    return (jnp.dot(tri, hi, preferred_element_type=F32) + jnp.dot(tri, mid, preferred_element_type=F32)
            + jnp.dot(tri, lo, preferred_element_type=F32))


def _tri_masks():
    r = lax.broadcasted_iota(jnp.int32, (CHUNK, CHUNK), 0)
    c = lax.broadcasted_iota(jnp.int32, (CHUNK, CHUNK), 1)
    return c <= r, c >= r


def _dt_comb(dt_raw, dtb_ref, alog_ref, valid_from=None):
    x = dt_raw + dtb_ref[...]
    sp = jnp.maximum(x, 0.0) + jnp.log(1.0 + jnp.exp(-jnp.abs(x)))
    lane = lax.broadcasted_iota(jnp.int32, sp.shape, 1)
    dtv = jnp.where(lane < 2 * N_SSM_HEADS, sp, 0.0)
    if valid_from is not None:
        row = lax.broadcasted_iota(jnp.int32, sp.shape, 0)
        dtv = jnp.where(row >= valid_from, dtv, 0.0)
    lane1 = lax.broadcasted_iota(jnp.int32, (1, DT_LANES), 1)
    a_mult = jnp.where((lane1 >= 2 * N_SSM_HEADS) & (lane1 < 4 * N_SSM_HEADS), -jnp.exp(alog_ref[...]), 0.0)
    return dtv + pltpu.roll(dtv, 2 * N_SSM_HEADS, axis=1) * a_mult


def _conv_silu(ext_ref, cw_ref, cb_ref):
    xe = ext_ref[...]
    n_ext = xe.shape[0]
    acc = cb_ref[...] + xe[CONV_HALO:CONV_HALO + CHUNK] * cw_ref[D_CONV // 2:D_CONV // 2 + 1, :]
    for j in range(D_CONV):
        if j != D_CONV // 2:
            tap = pltpu.roll(xe, (D_CONV // 2 - j) % n_ext, axis=0)[CONV_HALO:CONV_HALO + CHUNK]
            acc = acc + tap * cw_ref[j:j + 1, :]
    return acc * jax.nn.sigmoid(acc)


DT_F, DT_B = 0, N_SSM_HEADS
A_F, A_B = 2 * N_SSM_HEADS, 3 * N_SSM_HEADS


def _chunk_states(xs_t, comb_t, cum_t, b_mat, dt_row0, a_row0, end_col):
    dec = jnp.exp(cum_t[a_row0:a_row0 + N_SSM_HEADS, end_col:end_col + 1] - cum_t[a_row0:a_row0 + N_SSM_HEADS])
    out = []
    for g in range(N_SSM_GROUPS):
        rows = []
        for hh in range(HEADS_PER_GROUP):
            h = g * HEADS_PER_GROUP + hh
            xr = xs_t[HEAD_DIM * h:HEAD_DIM * (h + 1)] * (comb_t[dt_row0 + h:dt_row0 + h + 1] * dec[h:h + 1])
            rows.append(xr.astype(BF16))
        xrd = jnp.concatenate(rows, axis=0)
        out.append(jnp.dot(xrd, b_mat[:, D_STATE * g:D_STATE * (g + 1)].astype(BF16), preferred_element_type=F32))
    return out


def _update_state(state_ref, contrib, cum_t, a_row0, end_col):
    cd = jnp.exp(cum_t[a_row0:a_row0 + N_SSM_HEADS, end_col:end_col + 1])
    for g in range(N_SSM_GROUPS):
        for hh in range(HEADS_PER_GROUP):
            h = g * HEADS_PER_GROUP + hh
            sl = slice(HEAD_DIM * h, HEAD_DIM * (h + 1))
            state_ref[sl, :] = state_ref[sl, :] * cd[h:h + 1] + contrib[g][HEAD_DIM * hh:HEAD_DIM * (hh + 1)]


def _ssd_fwd_body(xc_ref, xp_ref, xn_ref, dt_ref, xm_ref, dtm_ref, cw_ref, cb_ref, dtb_ref, alog_ref, dskip_ref,
                  ypart_ref, xact_ref, state_ref, ext_ref):
    t = pl.program_id(1)
    ltri, utri = _tri_masks()
    ltri_b = jnp.where(ltri, 1.0, 0.0).astype(BF16)
    utri_b = jnp.where(utri, 1.0, 0.0).astype(BF16)

    @pl.when(t == 0)
    def _():
        ext_ref[0:CONV_HALO, :] = jnp.zeros((CONV_HALO, CONV_DIM), F32)
        ext_ref[CONV_HALO:CONV_HALO + CHUNK, :] = xm_ref[...]
        ext_ref[CONV_HALO + CHUNK:, :] = xc_ref[0:CONV_HALO, :]
        xact = _conv_silu(ext_ref, cw_ref, cb_ref)
        comb = _dt_comb(dtm_ref[...], dtb_ref, alog_ref, valid_from=META_LEAD)
        pc_t = _tri_dot(ltri_b, comb).T
        contrib = _chunk_states(xact[:, :SSM_WIDTH].T, comb.T, pc_t, xact[:, SSM_WIDTH:SSM_WIDTH + N_SSM_GROUPS * D_STATE],
                                DT_F, A_F, CHUNK - 1)
        for g in range(N_SSM_GROUPS):
            state_ref[HEADS_PER_GROUP * HEAD_DIM * g:HEADS_PER_GROUP * HEAD_DIM * (g + 1), :] = contrib[g]

    ext_ref[0:CONV_HALO, :] = jnp.where(t == 0, xm_ref[CHUNK - CONV_HALO:, :], xp_ref[...])
    ext_ref[CONV_HALO:CONV_HALO + CHUNK, :] = xc_ref[...]
    ext_ref[CONV_HALO + CHUNK:, :] = jnp.where(t == pl.num_programs(1) - 1, 0.0, xn_ref[...])
    xact = _conv_silu(ext_ref, cw_ref, cb_ref)
    xact_ref[...] = xact
    xs = xact[:, :SSM_WIDTH]
    b_mat = xact[:, SSM_WIDTH:SSM_WIDTH + N_SSM_GROUPS * D_STATE]
    c_mat = xact[:, SSM_WIDTH + N_SSM_GROUPS * D_STATE:]

    comb = _dt_comb(dt_ref[...], dtb_ref, alog_ref)
    pc = _tri_dot(ltri_b, comb)
    sc = _tri_dot(utri_b, comb)
    pc_t, sc_t, comb_t = pc.T, sc.T, comb.T
    xs_t = xs.T
    exp_p = jnp.exp(pc_t[A_F:A_F + N_SSM_HEADS])

    y_rows = []
    for g in range(N_SSM_GROUPS):
        bg = b_mat[:, D_STATE * g:D_STATE * (g + 1)].astype(BF16)
        cg_t = c_mat[:, D_STATE * g:D_STATE * (g + 1)].T.astype(BF16)
        cb_t = jnp.dot(bg, cg_t, preferred_element_type=F32)
        cb_f = jnp.where(utri, cb_t, 0.0)
        cb_b = jnp.where(ltri, cb_t, 0.0)
        st = state_ref[HEADS_PER_GROUP * HEAD_DIM * g:HEADS_PER_GROUP * HEAD_DIM * (g + 1), :].astype(BF16)
        yo_t = jnp.dot(st, cg_t, preferred_element_type=F32)
        for hh in range(HEADS_PER_GROUP):
            h = g * HEADS_PER_GROUP + hh
            ef = jnp.exp(jnp.minimum(pc_t[A_F + h:A_F + h + 1] - pc[:, A_F + h:A_F + h + 1], 0.0))
            eb = jnp.exp(jnp.minimum(sc_t[A_B + h:A_B + h + 1] - sc[:, A_B + h:A_B + h + 1], 0.0))
            m_t = jnp.concatenate([(cb_f * ef).astype(BF16), (cb_b * eb).astype(BF16)], axis=0)
            x_h = xs_t[HEAD_DIM * h:HEAD_DIM * (h + 1)]
            xr = jnp.concatenate([(x_h * comb_t[DT_F + h:DT_F + h + 1]).astype(BF16),
                                  (x_h * comb_t[DT_B + h:DT_B + h + 1]).astype(BF16)], axis=1)
            y_h = jnp.dot(xr, m_t, preferred_element_type=F32)
            y_rows.append(y_h + yo_t[HEAD_DIM * hh:HEAD_DIM * (hh + 1)] * exp_p[h:h + 1])
    y_t = jnp.concatenate(y_rows, axis=0)
    ypart_ref[...] = y_t.T + xs * dskip_ref[...]

    contrib = _chunk_states(xs_t, comb_t, pc_t, b_mat, DT_F, A_F, CHUNK - 1)
    _update_state(state_ref, contrib, pc_t, A_F, CHUNK - 1)


def _ssd_bwd_body(xact_ref, dt_ref, ypart_ref, z_ref, dtb_ref, alog_ref, gnorm_ref, o_ref, state_ref):
    t = pl.program_id(1)
    _, utri = _tri_masks()
    utri_b = jnp.where(utri, 1.0, 0.0).astype(BF16)

    @pl.when(t == 0)
    def _():
        state_ref[...] = jnp.zeros_like(state_ref)

    xact = xact_ref[...]
    xs = xact[:, :SSM_WIDTH]
    b_mat = xact[:, SSM_WIDTH:SSM_WIDTH + N_SSM_GROUPS * D_STATE]
    c_mat = xact[:, SSM_WIDTH + N_SSM_GROUPS * D_STATE:]
    comb = _dt_comb(dt_ref[...], dtb_ref, alog_ref)
    sc_t = _tri_dot(utri_b, comb).T
    comb_t = comb.T
    exp_s = jnp.exp(sc_t[A_B:A_B + N_SSM_HEADS])

    y_rows = []
    for g in range(N_SSM_GROUPS):
        cg_t = c_mat[:, D_STATE * g:D_STATE * (g + 1)].T.astype(BF16)
        st = state_ref[HEADS_PER_GROUP * HEAD_DIM * g:HEADS_PER_GROUP * HEAD_DIM * (g + 1), :].astype(BF16)
        yo_t = jnp.dot(st, cg_t, preferred_element_type=F32)
        for hh in range(HEADS_PER_GROUP):
            h = g * HEADS_PER_GROUP + hh
            y_rows.append(yo_t[HEAD_DIM * hh:HEAD_DIM * (hh + 1)] * exp_s[h:h + 1])
    y = ypart_ref[...] + jnp.concatenate(y_rows, axis=0).T
    z = z_ref[...]
    y = y * (z * jax.nn.sigmoid(z))
    gw = SSM_WIDTH // N_SSM_GROUPS
    for g in range(N_SSM_GROUPS):
        o_ref[:, gw * g:gw * (g + 1)] = _rms(y[:, gw * g:gw * (g + 1)], gnorm_ref[:, gw * g:gw * (g + 1)]).astype(BF16)

    contrib = _chunk_states(xs.T, comb_t, sc_t, b_mat, DT_B, A_B, 0)
    _update_state(state_ref, contrib, sc_t, A_B, 0)


def _ssd(xbc, dt_raw, z, xbc_meta, dt_meta, conv_w, conv_b, dtb, alog, dskip, gnorm, *, batch, seq):
    n = batch * seq
    nc = seq // CHUNK
    hb = CHUNK // CONV_HALO
    const = lambda b, t: (0, 0)
    ypart, xact = pl.pallas_call(
        _ssd_fwd_body,
        grid=(batch, nc),
        in_specs=[
            pl.BlockSpec((CHUNK, CONV_DIM), lambda b, t: (b * nc + t, 0)),
            pl.BlockSpec((CONV_HALO, CONV_DIM), lambda b, t: (jnp.maximum((b * nc + t) * hb - 1, 0), 0)),
            pl.BlockSpec((CONV_HALO, CONV_DIM), lambda b, t: (jnp.minimum((b * nc + t + 1) * hb, n // CONV_HALO - 1), 0)),
            pl.BlockSpec((CHUNK, DT_LANES), lambda b, t: (b * nc + t, 0)),
            pl.BlockSpec((META_ROWS, CONV_DIM), const),
            pl.BlockSpec((META_ROWS, DT_LANES), const),
            pl.BlockSpec((D_CONV, CONV_DIM), const),
            pl.BlockSpec((1, CONV_DIM), const),
            pl.BlockSpec((1, DT_LANES), const),
            pl.BlockSpec((1, DT_LANES), const),
            pl.BlockSpec((1, SSM_WIDTH), const),
        ],
        out_specs=[
            pl.BlockSpec((CHUNK, SSM_WIDTH), lambda b, t: (b * nc + t, 0)),
            pl.BlockSpec((CHUNK, CONV_DIM), lambda b, t: (b * nc + t, 0)),
        ],
        out_shape=[
            jax.ShapeDtypeStruct((n, SSM_WIDTH), F32),
            jax.ShapeDtypeStruct((n, CONV_DIM), F32),
        ],
        scratch_shapes=[
            pltpu.VMEM((N_SSM_HEADS * HEAD_DIM, D_STATE), F32),
            pltpu.VMEM((CHUNK + 2 * CONV_HALO, CONV_DIM), F32),
        ],
        compiler_params=_cparams(("parallel", "arbitrary")),
        name="ssd_fwd",
    )(xbc, xbc, xbc, dt_raw, xbc_meta, dt_meta, conv_w, conv_b, dtb, alog, dskip)

    rev = lambda b, t: (b * nc + nc - 1 - t, 0)
    return pl.pallas_call(
        _ssd_bwd_body,
        grid=(batch, nc),
        in_specs=[
            pl.BlockSpec((CHUNK, CONV_DIM), rev),
            pl.BlockSpec((CHUNK, DT_LANES), rev),
            pl.BlockSpec((CHUNK, SSM_WIDTH), rev),
            pl.BlockSpec((CHUNK, SSM_WIDTH), rev),
            pl.BlockSpec((1, DT_LANES), const),
            pl.BlockSpec((1, DT_LANES), const),
            pl.BlockSpec((1, SSM_WIDTH), const),
        ],
        out_specs=pl.BlockSpec((CHUNK, SSM_WIDTH), rev),
        out_shape=jax.ShapeDtypeStruct((n, SSM_WIDTH), BF16),
        scratch_shapes=[pltpu.VMEM((N_SSM_HEADS * HEAD_DIM, D_STATE), F32)],
        compiler_params=_cparams(("parallel", "arbitrary")),
        name="ssd_bwd",
    )(xact, dt_raw, ypart, z, dtb, alog, gnorm)


def _outproj_body(oa_ref, os_ref, h_ref, wa_ref, ws_ref, g_ref, o_ref):
    mix = (jnp.dot(oa_ref[...], wa_ref[...], preferred_element_type=F32)
           + jnp.dot(os_ref[...], ws_ref[...], preferred_element_type=F32))
    o_ref[...] = h_ref[...] + _rms(mix, g_ref[...])


def _outproj(o_att, o_ssm, h, w_att, w_ssm, gain, *, tm):
    n = h.shape[0]
    const = lambda i: (0, 0)
    return pl.pallas_call(
        _outproj_body,
        grid=(n // tm,),
        in_specs=[
            pl.BlockSpec((tm, ATT_WIDTH), lambda i: (i, 0)),
            pl.BlockSpec((tm, SSM_WIDTH), lambda i: (i, 0)),
            pl.BlockSpec((tm, D_MODEL), lambda i: (i, 0)),
            pl.BlockSpec((ATT_WIDTH, D_MODEL), const),
            pl.BlockSpec((SSM_WIDTH, D_MODEL), const),
            pl.BlockSpec((1, D_MODEL), const),
        ],
        out_specs=pl.BlockSpec((tm, D_MODEL), lambda i: (i, 0)),
        out_shape=jax.ShapeDtypeStruct((n, D_MODEL), F32),
        compiler_params=_cparams(("parallel",)),
        name="out_proj",
    )(o_att, o_ssm, h, w_att, w_ssm, gain)


def _rope_tables_t(n_tok):
    pos = jnp.arange(n_tok)
    row = (pos // GRID_W).astype(F32)
    col = (pos % GRID_W).astype(F32)
    n_freq = HEAD_DIM // 4
    inv_freq = ROPE_THETA ** (-jnp.arange(n_freq, dtype=F32) / n_freq)
    ang_r = inv_freq[:, None] * row[None, :]
    ang_c = inv_freq[:, None] * col[None, :]
    ang = jnp.concatenate([ang_r, ang_r, ang_c, ang_c], axis=0)
    sign = jnp.concatenate([-jnp.ones((n_freq, 1), F32), jnp.ones((n_freq, 1), F32)] * 2, axis=0)
    return jnp.cos(ang), jnp.sin(ang) * sign


def _tile_of(n, pref):
    return pref if n % pref == 0 else n


def kernel(x_prompt, x_sample, meta_tokens, ff1_norm_pre, ff1_w_gate, ff1_w_up, ff1_w_down, ff1_norm_post,
           mix_norm_pre, w_in, conv_w, conv_b, a_log, dt_bias, d_skip, q_norm, k_norm, ssm_norm, w_out,
           mix_norm_post, ff2_norm_pre, ff2_w_gate, ff2_w_up, ff2_w_down, ff2_norm_post):
    assert ff1_w_gate.shape[0] == 1, "single-layer trunk"
    row2 = lambda v: v.reshape(1, -1).astype(F32)
    ff1 = (row2(ff1_norm_pre[0]), ff1_w_gate[0].astype(BF16), ff1_w_up[0].astype(BF16), ff1_w_down[0].astype(BF16),
           row2(ff1_norm_post[0]))
    ff2 = (row2(ff2_norm_pre[0]), ff2_w_gate[0].astype(BF16), ff2_w_up[0].astype(BF16), ff2_w_down[0].astype(BF16),
           row2(ff2_norm_post[0]))
    w_in0 = w_in[0]
    wqkv_t = w_in0[:, :QKV_WIDTH].T.astype(BF16)
    dt_cols = w_in0[:, QKV_WIDTH + SSM_WIDTH + CONV_DIM:]
    wssm = jnp.concatenate([w_in0[:, QKV_WIDTH:QKV_WIDTH + SSM_WIDTH + CONV_DIM], dt_cols,
                            jnp.zeros((D_MODEL, DT_LANES - dt_cols.shape[1]), F32)], axis=1).astype(BF16)
    g_mix = row2(mix_norm_pre[0])
    gq = q_norm[0].reshape(HEAD_DIM, 1).astype(F32)
    gk = k_norm[0].reshape(HEAD_DIM, 1).astype(F32)
    n_dt = 2 * N_SSM_HEADS
    dtb = jnp.zeros((1, DT_LANES), F32).at[0, :n_dt].set(dt_bias[0].reshape(-1))
    alog = jnp.zeros((1, DT_LANES), F32).at[0, n_dt:2 * n_dt].set(a_log[0].reshape(-1))
    dskip = jnp.repeat(d_skip[0].astype(F32), HEAD_DIM).reshape(1, SSM_WIDTH)
    gnorm = row2(ssm_norm[0])
    cw = conv_w[0].astype(F32)
    cb = row2(conv_b[0])
    w_out_att = w_out[0][:ATT_WIDTH].astype(BF16)
    w_out_ssm = w_out[0][ATT_WIDTH:].astype(BF16)
    g_post = row2(mix_norm_post[0])

    def in_projection(h, cos_t, sin_t, tm):
        return _inproj(h, g_mix, wqkv_t, wssm, cos_t, sin_t, gq, gk, tm=tm)

    x_meta = jnp.concatenate([jnp.zeros((META_LEAD, D_MODEL), F32), meta_tokens.astype(F32)], axis=0)
    h_meta = _ffn(x_meta, *ff1, tm=META_ROWS)
    _, k_chunk, v_t_chunk, _, xbc_meta, dt_meta = in_projection(
        h_meta, jnp.ones((HEAD_DIM, META_ROWS), F32), jnp.zeros((HEAD_DIM, META_ROWS), F32), META_ROWS)
    k_meta = k_chunk[:, META_LEAD:, :]
    v_t_meta = jnp.pad(v_t_chunk[:, 0, :, META_LEAD:], ((0, 0), (0, 0), (0, META_LEAD)))

    def trunk(x):
        batch, seq, _ = x.shape
        n = batch * seq
        tm = _tile_of(n, ROW_TILE)
        tm_ff = _tile_of(n, FFN_ROW_TILE)
        h1 = _ffn(x.reshape(n, D_MODEL), *ff1, tm=tm_ff)
        cos_t, sin_t = _rope_tables_t(seq)
        tm_in = _tile_of(seq, ATT_TK)
        q_t, k, v_t, z, xbc, dt_raw = in_projection(h1, cos_t, sin_t, tm_in)
        o_att = _attention(q_t, k, v_t, k_meta, v_t_meta, batch=batch, seq=seq, tq=_tile_of(seq, ATT_TQ))
        o_ssm = _ssd(xbc, dt_raw, z, xbc_meta, dt_meta, cw, cb, dtb, alog, dskip, gnorm, batch=batch, seq=seq)
        h2 = _outproj(o_att, o_ssm, h1, w_out_att, w_out_ssm, g_post, tm=tm)
        return _ffn(h2, *ff2, tm=tm_ff).reshape(batch, seq, D_MODEL)

    return trunk(x_prompt), trunk(x_sample)
```
